```python
import math
import numpy as np
import jax
import jax.numpy as jnp
from jax import lax

D_MODEL = 1024
BATCH = 8
SEQ = 2048
DEPTH = 4
DEC_BATCH = 128
DEC_SEQ = 4
PAST_LEN = 2048
PAGE_SIZE = 128

HEAD_DIM = 64
GROUP_W = D_MODEL // 4
N_HEADS = GROUP_W // HEAD_DIM
DIFF_DQ = HEAD_DIM // 2
ROPE_THETA = 500000.0
QBLOCK = 128
CMP_BLOCK = 32
SEL_BLOCK = 64
SEL_TOPK = 8
WINDOW = 512
FORCE_SCORE = 1.0e4
POOL_WINDOWS = (2, 4, 8, 16)
POOL_GW = GROUP_W // len(POOL_WINDOWS)
POOL_HIST = max(POOL_WINDOWS) - 1
FFN_DIM = 2816
N_EXPERTS = 8
TOP_K_EXPERTS = 2
EXPERT_DIM = 1408
NORM_EPS = 1e-6
PROJ_SIZES = (
    N_HEADS * HEAD_DIM,
    HEAD_DIM, HEAD_DIM,
    HEAD_DIM, HEAD_DIM,
    HEAD_DIM, HEAD_DIM,
    3 * N_HEADS,
    N_HEADS * HEAD_DIM, N_HEADS * HEAD_DIM, N_HEADS * HEAD_DIM,
    N_HEADS * 2 * DIFF_DQ, N_HEADS * 2 * DIFF_DQ, N_HEADS * HEAD_DIM,
    GROUP_W,
)
IN_COLS = sum(PROJ_SIZES)

kernel_name = 'hybrid_nsa_stickbreak_diff_pool_decode_step'


def rmsnorm(x, g):
    xf = x.astype(jnp.float32)
    y = xf * lax.rsqrt(jnp.mean(xf * xf, axis=-1, keepdims=True) + NORM_EPS)
    return (y * g.astype(jnp.float32)).astype(x.dtype)


def rope_partial(x, pos):
    d = x.shape[-1]
    half = d // 8
    inv = ROPE_THETA ** (-jnp.arange(half, dtype=jnp.float32) / half)
    ang = pos.astype(jnp.float32)[:, None] * inv[None, :]
    cos = jnp.cos(ang)[:, None, :]
    sin = jnp.sin(ang)[:, None, :]
    xr = x[..., :2 * half].astype(jnp.float32)
    x1, x2 = xr[..., :half], xr[..., half:]
    rot = jnp.concatenate([x1 * cos - x2 * sin, x1 * sin + x2 * cos], axis=-1)
    return jnp.concatenate([rot.astype(x.dtype), x[..., 2 * half:]], axis=-1)


def masked_softmax(s, mask):
    s = jnp.where(mask, s.astype(jnp.float32), -jnp.inf)
    m = jnp.max(s, axis=-1, keepdims=True)
    m = jnp.where(jnp.isfinite(m), m, 0.0)
    e = jnp.where(mask, jnp.exp(s - m), 0.0)
    return e / jnp.maximum(jnp.sum(e, axis=-1, keepdims=True), 1e-30)


def block_means(rows):
    b, l, d = rows.shape
    return rows.reshape(b, l // CMP_BLOCK, CMP_BLOCK, d).mean(axis=2)


def gather_pages(cache_l, page_table):
    g = cache_l[page_table]
    return g.reshape((page_table.shape[0], page_table.shape[1] * cache_l.shape[1]) + cache_l.shape[2:])


def sweep_query_blocks(fn, arrays, n_q):
    def one(i):
        s0 = i * QBLOCK
        blocks = [lax.dynamic_slice_in_dim(a, s0, QBLOCK, axis=1) for a in arrays]
        return fn(s0 + jnp.arange(QBLOCK), *blocks)
    out = jnp.moveaxis(lax.map(one, jnp.arange(n_q // QBLOCK)), 0, 1)
    return out.reshape((out.shape[0], n_q) + out.shape[3:])


def project(h, w, pos):
    b, t, _ = h.shape
    cuts = np.cumsum(PROJ_SIZES)[:-1].tolist()
    (aq, akc, avc, aks, avs, akw, avw, ag, bq, bk, bv, cq, ck, cv, du) = jnp.split(h @ w, cuts, axis=-1)
    rope1 = lambda a: rope_partial(a[:, :, None, :], pos)[:, :, 0, :]
    heads = lambda a: a.reshape(b, t, N_HEADS, -1)
    diffqk = lambda a: rope_partial(a.reshape(b, t, 2 * N_HEADS, DIFF_DQ), pos).reshape(b, t, N_HEADS, 2, DIFF_DQ)
    return {
        'aq': rope_partial(heads(aq), pos), 'akc': rope1(akc), 'avc': avc,
        'aks': rope1(aks), 'avs': avs, 'akw': rope1(akw), 'avw': avw,
        'ag': jax.nn.sigmoid(ag.reshape(b, t, N_HEADS, 3)),
        'bq': heads(bq), 'bk': heads(bk), 'bv': heads(bv),
        'cq': diffqk(cq), 'ck': diffqk(ck), 'cv': heads(cv),
        'du': du,
    }


def nsa_core(q, g, qpos, kc, vc, ks, vs, kw, vw, kwpos):
    b, nq = q.shape[:2]
    scale = HEAD_DIM ** -0.5
    nc = kc.shape[1]
    ns = ks.shape[1] // SEL_BLOCK
    cend = (jnp.arange(nc) + 1) * CMP_BLOCK - 1
    cmask = cend[None, :] <= qpos[:, None]
    pc = masked_softmax(jnp.einsum('bqhd,bcd->bqhc', q, kc) * scale, cmask[None, :, None, :])
    o_c = jnp.einsum('bqhc,bcd->bqhd', pc.astype(vc.dtype), vc)
    imp = pc.sum(axis=2).reshape(b, nq, ns, SEL_BLOCK // CMP_BLOCK).sum(axis=-1)
    blk = jnp.arange(ns)[None, :]
    cur = (qpos // SEL_BLOCK)[:, None]
    forced = (blk == 0) | (blk == cur) | (blk == cur - 1)
    valid = blk * SEL_BLOCK <= qpos[:, None]
    imp = jnp.where(forced[None], FORCE_SCORE, imp)
    imp = jnp.where(valid[None], imp, -jnp.inf)
    _, top = lax.top_k(imp, min(SEL_TOPK, ns))
    tok = (top[..., None] * SEL_BLOCK + jnp.arange(SEL_BLOCK)).reshape(b, nq, -1)
    bidx = jnp.arange(b)[:, None, None]
    k_sel = ks[bidx, tok]
    v_sel = vs[bidx, tok]
    smask = tok <= qpos[None, :, None]
    ps = masked_softmax(jnp.einsum('bqhd,bqnd->bqhn', q, k_sel) * scale, smask[:, :, None, :])
    o_s = jnp.einsum('bqhn,bqnd->bqhd', ps.astype(vs.dtype), v_sel)
    wmask = (kwpos[None, :] <= qpos[:, None]) & (kwpos[None, :] > qpos[:, None] - WINDOW) & (kwpos[None, :] >= 0)
    pw = masked_softmax(jnp.einsum('bqhd,bkd->bqhk', q, kw) * scale, wmask[None, :, None, :])
    o_w = jnp.einsum('bqhk,bkd->bqhd', pw.astype(vw.dtype), vw)
    return g[..., 0:1] * o_c + g[..., 1:2] * o_s + g[..., 2:3] * o_w


def nsa_prompt(q, g, kc_rows, vc_rows, ks, vs, kw, vw):
    s = q.shape[1]
    kc = block_means(kc_rows)
    vc = block_means(vc_rows)
    kwp = jnp.pad(kw, ((0, 0), (WINDOW, 0), (0, 0)))
    vwp = jnp.pad(vw, ((0, 0), (WINDOW, 0), (0, 0)))

    def blk(qpos, qb, gb):
        s0 = qpos[0]
        kwb = lax.dynamic_slice_in_dim(kwp, s0, WINDOW + QBLOCK, axis=1)
        vwb = lax.dynamic_slice_in_dim(vwp, s0, WINDOW + QBLOCK, axis=1)
        kwpos = s0 - WINDOW + jnp.arange(WINDOW + QBLOCK)
        return nsa_core(qb, gb, qpos, kc, vc, ks, vs, kwb, vwb, kwpos)

    return sweep_query_blocks(blk, (q, g), s)


def nsa_sample(q, g, qpos, rows_new, past_rows, win_buf, kw, vw):
    t = q.shape[1]
    allr = jnp.concatenate([past_rows, rows_new], axis=1)
    length = allr.shape[1]
    lp = -(-length // SEL_BLOCK) * SEL_BLOCK
    allr = jnp.pad(allr, ((0, 0), (0, lp - length), (0, 0), (0, 0)))
    win = jnp.concatenate([win_buf, jnp.stack([kw, vw], axis=2)], axis=1)
    wb = win_buf.shape[1]
    kwpos = qpos[0] - wb + jnp.arange(wb + t)
    o = nsa_core(q, g, qpos, block_means(allr[:, :, 0]), block_means(allr[:, :, 1]),
                 allr[:, :, 2], allr[:, :, 3], win[:, :, 0], win[:, :, 1], kwpos)
    return o, win[:, -wb:]


def stickbreak_core(q, qpos, k, v, kpos):
    z = jnp.einsum('bqhd,bkhd->bhqk', q, k).astype(jnp.float32) * (HEAD_DIM ** -0.5)
    mask = kpos[None, :] < qpos[:, None]
    log_1m = jnp.where(mask, jax.nn.log_sigmoid(-z), 0.0)
    suffix = lax.cumsum(log_1m, axis=3, reverse=True) - log_1m
    a = jnp.where(mask, jnp.exp(jax.nn.log_sigmoid(z) + suffix), 0.0)
    return jnp.einsum('bhqk,bkhd->bqhd', a.astype(v.dtype), v)


def diff_core(q, qpos, k, v, kpos, lam, g_norm, lam_init):
    s = jnp.einsum('bqhmd,bkhmd->bhmqk', q, k) * (DIFF_DQ ** -0.5)
    mask = kpos[None, :] <= qpos[:, None]
    p = masked_softmax(s, mask)
    w = p[:, :, 0] - lam * p[:, :, 1]
    o = jnp.einsum('bhqk,bkhd->bqhd', w.astype(v.dtype), v)
    return rmsnorm(o, g_norm) * (1.0 - lam_init)


def pool_mix(u_ext, pos_out, w_pool, scale):
    n_out = pos_out.shape[0]
    up = jnp.pad(u_ext.astype(jnp.float32), ((0, 0), (max(POOL_WINDOWS), 0), (0, 0)))
    cs = jnp.cumsum(up, axis=1)
    n = up.shape[1]
    cur = up[:, n - n_out:]
    outs = []
    for gi, w in enumerate(POOL_WINDOWS):
        sl = slice(gi * POOL_GW, (gi + 1) * POOL_GW)
        wsum = cs[:, n - n_out:, sl] - cs[:, n - n_out - w:n - w, sl]
        mean = wsum / jnp.minimum(w, pos_out + 1).astype(jnp.float32)[None, :, None]
        outs.append(jnp.einsum('btc,cd->btd', (mean - cur[..., sl]).astype(u_ext.dtype), w_pool[gi]))
    return jnp.concatenate(outs, axis=-1) * scale


def swiglu(h, wg, wu, wd):
    return (jax.nn.silu(h @ wg) * (h @ wu)) @ wd


def moe_ffn(h, w_r, wg, wu, wd):
    logits = jnp.einsum('btd,de->bte', h, w_r).astype(jnp.float32)
    top_v, top_i = lax.top_k(logits, TOP_K_EXPERTS)
    gate = jnp.sum(jax.nn.one_hot(top_i, N_EXPERTS, dtype=jnp.float32)
                   * jax.nn.softmax(top_v, axis=-1)[..., None], axis=-2)
    y = jnp.zeros_like(h)
    for e in range(N_EXPERTS):
        y = y + gate[..., e:e + 1].astype(h.dtype) * swiglu(h, wg[e], wu[e], wd[e])
    return y


def channel_mix(h, l, ffn_gate, ffn_up, ffn_down, router_w, moe_gate, moe_up, moe_down):
    i = l // 2
    if l % 2 == 0:
        return swiglu(h, ffn_gate[i], ffn_up[i], ffn_down[i])
    return moe_ffn(h, router_w[i], moe_gate[i], moe_up[i], moe_down[i])


def merge_heads(oa, ob, oc, od):
    b, t = od.shape[:2]
    return jnp.concatenate([oa.reshape(b, t, GROUP_W), ob.reshape(b, t, GROUP_W),
                            oc.reshape(b, t, GROUP_W), od], axis=-1)


def setup_inputs(seed: int = 0) -> dict:
    key = jax.random.key(seed)
    k = jax.random.split(key, 24)
    f32 = jnp.float32

    def nrm(kk, shape, scale):
        return jax.random.normal(kk, shape, f32) * scale

    n_pages = PAST_LEN // PAGE_SIZE
    n_used = DEC_BATCH * n_pages
    n_pool = n_used + n_used // 4
    n_dense = (DEPTH + 1) // 2
    n_moe = DEPTH // 2
    w_buf = min(WINDOW, PAST_LEN)
    page_table = jax.random.permutation(k[0], n_pool)[:n_used].reshape(DEC_BATCH, n_pages).astype(jnp.int32)
    return {
        'x_prompt': nrm(k[1], (BATCH, SEQ, D_MODEL), 1.0),
        'x_sample': nrm(k[2], (DEC_BATCH, DEC_SEQ, D_MODEL), 1.0),
        'cache_nsa': nrm(k[3], (DEPTH, n_pool, PAGE_SIZE, 4, HEAD_DIM), 1.0),
        'cache_sb': nrm(k[4], (DEPTH, n_pool, PAGE_SIZE, 2, N_HEADS, HEAD_DIM), 1.0),
        'cache_diff': nrm(k[5], (DEPTH, n_pool, PAGE_SIZE, 2, N_HEADS, HEAD_DIM), 1.0),
        'state_nsa_win': nrm(k[6], (DEPTH, DEC_BATCH, w_buf, 2, HEAD_DIM), 1.0),
        'state_pool': nrm(k[7], (DEPTH, DEC_BATCH, POOL_HIST, GROUP_W), 1.0),
        'page_table': page_table,
        'norm_mix': 1.0 + nrm(k[8], (DEPTH, D_MODEL), 0.02),
        'w_in': nrm(k[9], (DEPTH, D_MODEL, IN_COLS), D_MODEL ** -0.5),
        'w_out': nrm(k[10], (DEPTH, D_MODEL, D_MODEL), D_MODEL ** -0.5),
        'diff_lambda': nrm(k[11], (DEPTH, 4, DIFF_DQ), 0.1),
        'diff_norm': 1.0 + nrm(k[12], (DEPTH, HEAD_DIM), 0.02),
        'pool_w': nrm(k[13], (DEPTH, len(POOL_WINDOWS), POOL_GW, POOL_GW), POOL_GW ** -0.5),
        'pool_scale': 1.0 + nrm(k[14], (DEPTH, GROUP_W), 0.1),
        'norm_ffn': 1.0 + nrm(k[15], (DEPTH, D_MODEL), 0.02),
        'ffn_gate': nrm(k[16], (n_dense, D_MODEL, FFN_DIM), D_MODEL ** -0.5),
        'ffn_up': nrm(k[17], (n_dense, D_MODEL, FFN_DIM), D_MODEL ** -0.5),
        'ffn_down': nrm(k[18], (n_dense, FFN_DIM, D_MODEL), FFN_DIM ** -0.5),
        'router_w': nrm(k[19], (n_moe, D_MODEL, N_EXPERTS), D_MODEL ** -0.5),
        'moe_gate': nrm(k[20], (n_moe, N_EXPERTS, D_MODEL, EXPERT_DIM), D_MODEL ** -0.5),
        'moe_up': nrm(k[21], (n_moe, N_EXPERTS, D_MODEL, EXPERT_DIM), D_MODEL ** -0.5),
        'moe_down': nrm(k[22], (n_moe, N_EXPERTS, EXPERT_DIM, D_MODEL), EXPERT_DIM ** -0.5),
        'norm_final': 1.0 + nrm(k[23], (D_MODEL,), 0.02),
    }


def reference(x_prompt, x_sample, cache_nsa, cache_sb, cache_diff, state_nsa_win, state_pool, page_table,
              norm_mix, w_in, w_out, diff_lambda, diff_norm, pool_w, pool_scale, norm_ffn,
              ffn_gate, ffn_up, ffn_down, router_w, moe_gate, moe_up, moe_down, norm_final):
    b, s, _ = x_prompt.shape
    db, t, _ = x_sample.shape
    past = page_table.shape[1] * PAGE_SIZE
    pos_p = jnp.arange(s)
    pos_s = past + jnp.arange(t)
    kpos_s = jnp.arange(past + t)
    xp, xs = x_prompt, x_sample
    nsa_p, nsa_s, sb_p, sb_s, df_p, df_s, win_p, win_s, pl_p, pl_s = ([] for _ in range(10))
    for l in range(DEPTH):
        lam_init = 0.8 - 0.6 * math.exp(-0.3 * l)
        lv = diff_lambda[l].astype(jnp.float32)
        lam = jnp.exp(jnp.sum(lv[0] * lv[1])) - jnp.exp(jnp.sum(lv[2] * lv[3])) + lam_init
        fp = project(rmsnorm(xp, norm_mix[l]), w_in[l], pos_p)
        fs = project(rmsnorm(xs, norm_mix[l]), w_in[l], pos_s)

        rows_p = jnp.stack([fp['akc'], fp['avc'], fp['aks'], fp['avs']], axis=2)
        rows_s = jnp.stack([fs['akc'], fs['avc'], fs['aks'], fs['avs']], axis=2)
        oa_p = nsa_prompt(fp['aq'], fp['ag'], fp['akc'], fp['avc'], fp['aks'], fp['avs'], fp['akw'], fp['avw'])
        oa_s, wbuf_s = nsa_sample(fs['aq'], fs['ag'], pos_s, rows_s, gather_pages(cache_nsa[l], page_table),
                                  state_nsa_win[l], fs['akw'], fs['avw'])
        nsa_p.append(rows_p)
        nsa_s.append(rows_s)
        win_p.append(jnp.stack([fp['akw'], fp['avw']], axis=2)[:, -min(WINDOW, s):])
        win_s.append(wbuf_s)

        ob_p = sweep_query_blocks(lambda qpos, qb: stickbreak_core(qb, qpos, fp['bk'], fp['bv'], pos_p), (fp['bq'],), s)
        sbc = gather_pages(cache_sb[l], page_table)
        ob_s = stickbreak_core(fs['bq'], pos_s, jnp.concatenate([sbc[:, :, 0], fs['bk']], axis=1),
                               jnp.concatenate([sbc[:, :, 1], fs['bv']], axis=1), kpos_s)
        sb_p.append(jnp.stack([fp['bk'], fp['bv']], axis=2))
        sb_s.append(jnp.stack([fs['bk'], fs['bv']], axis=2))

        oc_p = sweep_query_blocks(lambda qpos, qb: diff_core(qb, qpos, fp['ck'], fp['cv'], pos_p, lam, diff_norm[l], lam_init),
                                  (fp['cq'],), s)
        dfc = gather_pages(cache_diff[l], page_table)
        kc_all = jnp.concatenate([dfc[:, :, 0].reshape(db, past, N_HEADS, 2, DIFF_DQ), fs['ck']], axis=1)
        vc_all = jnp.concatenate([dfc[:, :, 1], fs['cv']], axis=1)
        oc_s = diff_core(fs['cq'], pos_s, kc_all, vc_all, kpos_s, lam, diff_norm[l], lam_init)
        df_p.append(jnp.stack([fp['ck'].reshape(b, s, N_HEADS, HEAD_DIM), fp['cv']], axis=2))
        df_s.append(jnp.stack([fs['ck'].reshape(db, t, N_HEADS, HEAD_DIM), fs['cv']], axis=2))

        od_p = pool_mix(fp['du'], pos_p, pool_w[l], pool_scale[l])
        u_ext = jnp.concatenate([state_pool[l], fs['du']], axis=1)
        od_s = pool_mix(u_ext, pos_s, pool_w[l], pool_scale[l])
        pl_p.append(fp['du'][:, -POOL_HIST:])
        pl_s.append(u_ext[:, -POOL_HIST:])

        xp = xp + merge_heads(oa_p, ob_p, oc_p, od_p) @ w_out[l]
        xs = xs + merge_heads(oa_s, ob_s, oc_s, od_s) @ w_out[l]
        xp = xp + channel_mix(rmsnorm(xp, norm_ffn[l]), l, ffn_gate, ffn_up, ffn_down, router_w, moe_gate, moe_up, moe_down)
        xs = xs + channel_mix(rmsnorm(xs, norm_ffn[l]), l, ffn_gate, ffn_up, ffn_down, router_w, moe_gate, moe_up, moe_down)

    y_prompt = rmsnorm(xp, norm_final)
    y_sample = rmsnorm(xs, norm_final)
    return (y_prompt, y_sample,
            jnp.stack(nsa_p), jnp.stack(nsa_s), jnp.stack(sb_p), jnp.stack(sb_s),
            jnp.stack(df_p), jnp.stack(df_s), jnp.stack(win_p), jnp.stack(win_s),
            jnp.stack(pl_p), jnp.stack(pl_s))
```

```python
import functools
import math

import numpy as np
import jax
import jax.numpy as jnp
from jax import lax
from jax.experimental import pallas as pl
from jax.experimental.pallas import tpu as pltpu

F32 = jnp.float32
BF16 = jnp.bfloat16
I32 = jnp.int32

D_MODEL = 1024
HEAD_DIM = 64
GROUP_W = D_MODEL // 4
N_HEADS = GROUP_W // HEAD_DIM
DIFF_DQ = HEAD_DIM // 2
ROPE_THETA = 500000.0
CMP_BLOCK = 32
SEL_BLOCK = 64
SEL_TOPK = 8
WINDOW = 512
FORCE_SCORE = 1.0e4
POOL_WINDOWS = (2, 4, 8, 16)
POOL_GW = GROUP_W // len(POOL_WINDOWS)
POOL_HIST = max(POOL_WINDOWS) - 1
N_EXPERTS = 8
NORM_EPS = 1e-6
PROJ_SIZES = (256, 64, 64, 64, 64, 64, 64, 12, 256, 256, 256, 256, 256, 256, 256)

QBLOCK = 128
LANES = 128
SUBLANES = 8
VMEM_LIMIT = 56 * 1024 * 1024

C_AQ, C_ROWS, C_WIN, C_GATE, C_BQ, C_BKV, C_CQ, C_CKV, C_DU, C_END = (
    0, 256, 512, 640, 768, 1024, 1536, 1792, 2304, 2560)

NEG_INF = float("-inf")


def _dot(a, b):
    return jnp.dot(a, b, preferred_element_type=F32)


def _dot_nt(a, b):
    return lax.dot_general(a, b, (((1,), (1,)), ((), ())), preferred_element_type=F32)


def _split_bf16(x, parts):
    out = []
    r = x
    for _ in range(parts):
        p = r.astype(BF16)
        out.append(p)
        r = r - p.astype(F32)
    return out


def _dot_split_lhs(x, w, parts):
    acc = None
    for p in _split_bf16(x, parts):
        t = _dot(p, w)
        acc = t if acc is None else acc + t
    return acc


def _dot_split_rhs(w, x, parts):
    acc = None
    for p in _split_bf16(x, parts):
        t = _dot(w, p)
        acc = t if acc is None else acc + t
    return acc


def _iota(shape, dim):
    return lax.broadcasted_iota(I32, shape, dim)


def _rmsnorm(x, g):
    ms = jnp.mean(x * x, axis=-1, keepdims=True)
    return x * lax.rsqrt(ms + NORM_EPS) * g


def _softmax_parts(s, mask):
    s = jnp.where(mask, s, NEG_INF)
    m = jnp.max(s, axis=-1, keepdims=True)
    m = jnp.where(m == NEG_INF, 0.0, m)
    e = jnp.where(mask, jnp.exp(s - m), 0.0)
    l = jnp.maximum(jnp.sum(e, axis=-1, keepdims=True), 1e-30)
    return e, l


def _softplus(z):
    return jnp.maximum(z, 0.0) + jnp.log1p(jnp.exp(-jnp.abs(z)))


def _silu(x):
    return x / (1.0 + jnp.exp(-x))


def _params(*sem):
    return pltpu.CompilerParams(dimension_semantics=sem, vmem_limit_bytes=VMEM_LIMIT)


def _const_spec(shape):
    nd = len(shape)
    return pl.BlockSpec(shape, lambda *_: (0,) * nd)


def _rope_tile(f, c, s1, s2, shift):
    return f * c + pltpu.roll(f, shift, 1) * s1 + pltpu.roll(f, LANES - shift, 1) * s2


def _proj_kernel(x_ref, g_ref, w_ref, ca_ref, sa1_ref, sa2_ref, cc_ref, sc1_ref, sc2_ref,
                 qa_ref, rows_ref, win_ref, gate_ref, qb_ref, kvb_ref, qc_ref, kvc_ref, du_ref):
    h = _rmsnorm(x_ref[...], g_ref[...]).astype(BF16)

    def mm(c0, c1):
        return _dot(h, w_ref[:, c0:c1])

    ca, sa1, sa2 = ca_ref[...], sa1_ref[...], sa2_ref[...]
    cc, sc1, sc2 = cc_ref[...], sc1_ref[...], sc2_ref[...]
    first = _iota(ca.shape, 1) < HEAD_DIM
    cb = jnp.where(first, ca, 1.0)
    sb1 = jnp.where(first, sa1, 0.0)
    sb2 = jnp.where(first, sa2, 0.0)
    ha = HEAD_DIM // 8
    hc = DIFF_DQ // 8

    for t in range(2):
        f = mm(C_AQ + t * LANES, C_AQ + (t + 1) * LANES)
        qa_ref[:, t * LANES:(t + 1) * LANES] = _rope_tile(f, ca, sa1, sa2, ha).astype(BF16)
    for t in range(2):
        f = mm(C_ROWS + t * LANES, C_ROWS + (t + 1) * LANES)
        rows_ref[:, t * LANES:(t + 1) * LANES] = _rope_tile(f, cb, sb1, sb2, ha)
    win_ref[...] = _rope_tile(mm(C_WIN, C_GATE), cb, sb1, sb2, ha)
    gate_ref[...] = 1.0 / (1.0 + jnp.exp(-mm(C_GATE, C_BQ)))
    qb_ref[...] = mm(C_BQ, C_BKV).astype(BF16)
    kvb_ref[...] = mm(C_BKV, C_CQ)
    for t in range(2):
        f = mm(C_CQ + t * LANES, C_CQ + (t + 1) * LANES)
        qc_ref[:, t * LANES:(t + 1) * LANES] = _rope_tile(f, cc, sc1, sc2, hc).astype(BF16)
    for t in range(2):
        f = mm(C_CKV + t * LANES, C_CKV + (t + 1) * LANES)
        kvc_ref[:, t * LANES:(t + 1) * LANES] = _rope_tile(f, cc, sc1, sc2, hc)
    kvc_ref[:, 2 * LANES:4 * LANES] = mm(C_CKV + 2 * LANES, C_DU)
    du_ref[...] = mm(C_DU, C_END)


def _project(x, g, w, tabs, tm):
    n = x.shape[0]
    p = tabs[0].shape[0]
    nper = p // tm
    row = lambda c: pl.BlockSpec((tm, c), lambda i: (i, 0))
    tab = pl.BlockSpec((tm, LANES), lambda i: (i % nper, 0))
    outs = [(256, BF16), (256, F32), (128, F32), (128, F32), (256, BF16), (512, F32),
            (256, BF16), (512, F32), (256, F32)]
    return pl.pallas_call(
        _proj_kernel,
        grid=(n // tm,),
        in_specs=[row(D_MODEL), _const_spec((1, D_MODEL)), _const_spec((D_MODEL, C_END))] + [tab] * 6,
        out_specs=[row(c) for c, _ in outs],
        out_shape=[jax.ShapeDtypeStruct((n, c), d) for c, d in outs],
        compiler_params=_params("parallel"),
        name="proj",
    )(x, g, w, *tabs)


def _rope_tables(pos):
    pos = pos.astype(F32)[:, None]
    lane = np.arange(LANES)
    out = []
    for width in (HEAD_DIM, DIFF_DQ):
        half = width // 8
        d = lane % width
        inv = ROPE_THETA ** (-jnp.arange(half, dtype=F32) / half)
        ang = pos * inv[None, :]
        cos, sin = jnp.cos(ang), jnp.sin(ang)
        is1 = d < half
        is2 = (d >= half) & (d < 2 * half)
        fidx = np.where(is1, d, np.where(is2, d - half, 0))
        cosl, sinl = cos[:, fidx], sin[:, fidx]
        rot = jnp.asarray(is1 | is2)[None, :]
        out.append(jnp.where(rot, cosl, 1.0))
        out.append(jnp.where(jnp.asarray(is2)[None, :], sinl, 0.0))
        out.append(jnp.where(jnp.asarray(is1)[None, :], -sinl, 0.0))
    return out


def _prep_w_in(w):
    cuts = np.cumsum(PROJ_SIZES)[:-1].tolist()
    (aq, akc, avc, aks, avs, akw, avw, ag, bq, bk, bv, cq, ck, cv, du) = jnp.split(w, cuts, axis=-1)
    agp = jnp.pad(ag, ((0, 0), (0, LANES - ag.shape[1])))
    return jnp.concatenate([aq, akc, avc, aks, avs, akw, avw, agp, bq, bk, bv, cq, ck, cv, du],
                           axis=-1).astype(BF16)


def _select_blocks(imp, qpos, ns):
    w = imp.shape[1]
    blk = _iota((1, w), 1)
    cur = qpos >> 6
    forced = (blk == 0) | (blk == cur) | (blk == cur - 1)
    valid = (blk * SEL_BLOCK <= qpos) & (blk < ns)
    imp = jnp.where(forced, FORCE_SCORE, imp)
    imp = jnp.where(valid, imp, NEG_INF)
    rank = jnp.zeros(imp.shape, F32)
    for i in range(ns):
        ci = imp[:, i:i + 1]
        tie = jnp.where(blk > i, 1.0, 0.0)
        rank = rank + jnp.where(ci > imp, 1.0, jnp.where(ci == imp, tie, 0.0))
    return jnp.where((rank < min(SEL_TOPK, ns)) & (blk < ns), 1.0, 0.0)


def _nsa_prompt_kernel(q_ref, g_ref, rows_ref, win_ref, o_ref,
                       kcvc_ref, ks_ref, vs_ref, kw_ref, vw_ref, *, seq):
    qi = pl.program_id(1)
    s0 = qi * QBLOCK
    nc = seq // CMP_BLOCK
    ns = seq // SEL_BLOCK
    scale = HEAD_DIM ** -0.5

    @pl.when(qi == 0)
    def _():
        r = _iota((nc, seq), 0)
        t = _iota((nc, seq), 1)
        pm = jnp.where((t >> 5) == r, 1.0 / CMP_BLOCK, 0.0).astype(BF16)
        kcvc_ref[...] = _dot_split_rhs(pm, rows_ref[0, :, 0:LANES], 2)
        ks_ref[...] = rows_ref[0, :, 128:192].astype(BF16)
        vs_ref[...] = rows_ref[0, :, 192:256].astype(BF16)
        kw_ref[...] = win_ref[0, :, 0:64].astype(BF16)
        vw_ref[...] = win_ref[0, :, 64:128].astype(BF16)

    q = q_ref[0]
    g = g_ref[0]
    qpos = s0 + _iota((QBLOCK, 1), 0)

    q4 = jnp.concatenate([q[:, HEAD_DIM * h:HEAD_DIM * (h + 1)] for h in range(N_HEADS)], axis=0)
    qpos4 = jnp.concatenate([qpos] * N_HEADS, axis=0)
    kc = kcvc_ref[:, 0:HEAD_DIM].astype(BF16)
    vc = kcvc_ref[:, HEAD_DIM:2 * HEAD_DIM].astype(BF16)
    cend = (_iota((1, nc), 1) + 1) * CMP_BLOCK - 1
    e, l = _softmax_parts(_dot_nt(q4, kc) * scale, cend <= qpos4)
    pc = e / l
    oc = _dot(pc.astype(BF16), vc)
    pcs = pc[0:QBLOCK]
    for h in range(1, N_HEADS):
        pcs = pcs + pc[h * QBLOCK:(h + 1) * QBLOCK]
    pair = jnp.where((_iota((nc, ns), 0) >> 1) == _iota((nc, ns), 1), 1.0, 0.0).astype(BF16)
    imp = _dot_split_lhs(pcs, pair, 3)
    sel = _select_blocks(imp, qpos, ns)

    expand = jnp.where(_iota((ns, seq), 0) == (_iota((ns, seq), 1) >> 6), 1.0, 0.0).astype(BF16)
    kpos = _iota((1, seq), 1)
    smask = (_dot(sel.astype(BF16), expand) > 0.5) & (kpos <= qpos)

    wl = WINDOW + QBLOCK
    start = pl.multiple_of(jnp.maximum(s0 - WINDOW, 0), QBLOCK)
    kw = kw_ref[pl.ds(start, wl), :]
    vw = vw_ref[pl.ds(start, wl), :]
    kwpos = start + _iota((1, wl), 1)
    wmask = (kwpos <= qpos) & (kwpos > qpos - WINDOW)

    ks = ks_ref[...]
    vs = vs_ref[...]
    outs = []
    for h in range(N_HEADS):
        qh = q[:, HEAD_DIM * h:HEAD_DIM * (h + 1)]
        e, l = _softmax_parts(_dot_nt(qh, ks) * scale, smask)
        o_s = _dot(e.astype(BF16), vs) / l
        e, l = _softmax_parts(_dot_nt(qh, kw) * scale, wmask)
        o_w = _dot(e.astype(BF16), vw) / l
        o_c = oc[h * QBLOCK:(h + 1) * QBLOCK]
        outs.append(g[:, 3 * h:3 * h + 1] * o_c + g[:, 3 * h + 1:3 * h + 2] * o_s
                    + g[:, 3 * h + 2:3 * h + 3] * o_w)
    o_ref[0] = jnp.concatenate(outs, axis=1).astype(BF16)


def _nsa_prompt(q, g, rows, win):
    b, s, _ = q.shape
    qspec = lambda c: pl.BlockSpec((1, QBLOCK, c), lambda i, j: (i, j, 0))
    full = lambda c: pl.BlockSpec((1, s, c), lambda i, j: (i, 0, 0))
    return pl.pallas_call(
        functools.partial(_nsa_prompt_kernel, seq=s),
        grid=(b, s // QBLOCK),
        in_specs=[qspec(256), qspec(128), full(256), full(128)],
        out_specs=qspec(256),
        out_shape=jax.ShapeDtypeStruct((b, s, GROUP_W), BF16),
        scratch_shapes=[pltpu.VMEM((s // CMP_BLOCK, LANES), F32)]
        + [pltpu.VMEM((s, HEAD_DIM), BF16)] * 4,
        compiler_params=_params("parallel", "arbitrary"),
        name="nsa_prompt",
    )(q, g, rows, win)


def _sb_prompt_kernel(q_ref, kv_ref, o_ref, kb_ref, vb_ref, r_ref, acc_ref):
    qi = pl.program_id(1)
    s0 = qi * QBLOCK
    scale = HEAD_DIM ** -0.5

    @pl.when(qi == 0)
    def _():
        kb_ref[...] = kv_ref[0, :, 0:GROUP_W].astype(BF16)
        vb_ref[...] = kv_ref[0, :, GROUP_W:2 * GROUP_W].astype(BF16)

    q = q_ref[0]
    r_ref[...] = jnp.zeros(r_ref.shape, F32)
    acc_ref[...] = jnp.zeros(acc_ref.shape, F32)
    later = jnp.where(_iota((QBLOCK, QBLOCK), 0) > _iota((QBLOCK, QBLOCK), 1), 1.0, 0.0).astype(BF16)
    qpos = s0 + _iota((QBLOCK, 1), 0)

    def body(it, carry):
        k0 = pl.multiple_of((qi - it) * QBLOCK, QBLOCK)
        mask = (k0 + _iota((1, QBLOCK), 1)) < qpos
        for h in range(N_HEADS):
            cs = slice(HEAD_DIM * h, HEAD_DIM * (h + 1))
            z = _dot_nt(q[:, cs], kb_ref[pl.ds(k0, QBLOCK), cs]) * scale
            sp = _softplus(z)
            l1m = jnp.where(mask, -sp, 0.0)
            suf = _dot_split_lhs(l1m, later, 2)
            a = jnp.where(mask, jnp.exp(z - sp + suf + r_ref[h]), 0.0)
            acc_ref[:, cs] += _dot(a.astype(BF16), vb_ref[pl.ds(k0, QBLOCK), cs])
            r_ref[h] = r_ref[h] + (suf[:, 0:1] + l1m[:, 0:1])
        return carry

    lax.fori_loop(0, qi + 1, body, 0)
    o_ref[0] = acc_ref[...].astype(BF16)


def _sb_prompt(q, kv):
    b, s, _ = q.shape
    qspec = pl.BlockSpec((1, QBLOCK, GROUP_W), lambda i, j: (i, j, 0))
    return pl.pallas_call(
        _sb_prompt_kernel,
        grid=(b, s // QBLOCK),
        in_specs=[qspec, pl.BlockSpec((1, s, 2 * GROUP_W), lambda i, j: (i, 0, 0))],
        out_specs=qspec,
        out_shape=jax.ShapeDtypeStruct((b, s, GROUP_W), BF16),
        scratch_shapes=[pltpu.VMEM((s, GROUP_W), BF16), pltpu.VMEM((s, GROUP_W), BF16),
                        pltpu.VMEM((N_HEADS, QBLOCK, QBLOCK), F32), pltpu.VMEM((QBLOCK, GROUP_W), F32)],
        compiler_params=_params("parallel", "arbitrary"),
        name="sb_prompt",
    )(q, kv)


def _diff_lambda(lam_ref, lam_init):
    lv = lam_ref[...]
    a = jnp.sum(lv[0:1] * lv[1:2], axis=1, keepdims=True)
    b = jnp.sum(lv[2:3] * lv[3:4], axis=1, keepdims=True)
    return jnp.exp(a) - jnp.exp(b) + lam_init


def _diff_finish(o1, o2, lam, gn, lam_init):
    o = o1 - lam * o2
    return _rmsnorm(o, gn) * (1.0 - lam_init)


def _diff_prompt_kernel(q_ref, kv_ref, lam_ref, gn_ref, o_ref, kb_ref, vb_ref, *, seq, lam_init):
    qi = pl.program_id(1)
    s0 = qi * QBLOCK
    scale = DIFF_DQ ** -0.5

    @pl.when(qi == 0)
    def _():
        kb_ref[...] = kv_ref[0, :, 0:GROUP_W].astype(BF16)
        vb_ref[...] = kv_ref[0, :, GROUP_W:2 * GROUP_W].astype(BF16)

    lam = _diff_lambda(lam_ref, lam_init)
    gn = gn_ref[...]
    q = q_ref[0]
    mask = _iota((1, seq), 1) <= s0 + _iota((QBLOCK, 1), 0)
    outs = []
    for h in range(N_HEADS):
        v = vb_ref[:, HEAD_DIM * h:HEAD_DIM * (h + 1)]
        o = []
        for m in range(2):
            cs = slice(HEAD_DIM * h + DIFF_DQ * m, HEAD_DIM * h + DIFF_DQ * (m + 1))
            e, l = _softmax_parts(_dot_nt(q[:, cs], kb_ref[:, cs]) * scale, mask)
            o.append(_dot(e.astype(BF16), v) / l)
        outs.append(_diff_finish(o[0], o[1], lam, gn, lam_init))
    o_ref[0] = jnp.concatenate(outs, axis=1).astype(BF16)


def _diff_prompt(q, kv, lam, gn, lam_init):
    b, s, _ = q.shape
    qspec = pl.BlockSpec((1, QBLOCK, GROUP_W), lambda i, j: (i, j, 0))
    return pl.pallas_call(
        functools.partial(_diff_prompt_kernel, seq=s, lam_init=lam_init),
        grid=(b, s // QBLOCK),
        in_specs=[qspec, pl.BlockSpec((1, s, 2 * GROUP_W), lambda i, j: (i, 0, 0)),
                  pl.BlockSpec((4, DIFF_DQ), lambda i, j: (0, 0)),
                  pl.BlockSpec((1, HEAD_DIM), lambda i, j: (0, 0))],
        out_specs=qspec,
        out_shape=jax.ShapeDtypeStruct((b, s, GROUP_W), BF16),
        scratch_shapes=[pltpu.VMEM((s, GROUP_W), BF16), pltpu.VMEM((s, GROUP_W), BF16)],
        compiler_params=_params("parallel", "arbitrary"),
        name="diff_prompt",
    )(q, kv, lam, gn)


def _pool_kernel(u_ref, w_ref, sc_ref, o_ref, buf_ref, *, hist, pos0):
    n_ext = u_ref.shape[1]
    n_out = n_ext - hist
    pad = max(POOL_WINDOWS)
    u = u_ref[0]
    buf_ref[0:pad, :] = jnp.zeros((pad, GROUP_W), F32)
    buf_ref[pad:pad + n_ext, :] = u
    acc = u
    sums = []
    shift = 1
    for _ in POOL_WINDOWS:
        acc = acc + buf_ref[pad - shift:pad - shift + n_ext, :]
        sums.append(acc)
        shift *= 2
        if shift < pad:
            buf_ref[pad:pad + n_ext, :] = acc
    lane = _iota((1, GROUP_W), 1)
    grp = lane >> 6
    wsum = jnp.where(grp == 0, sums[0], jnp.where(grp == 1, sums[1], jnp.where(grp == 2, sums[2], sums[3])))
    wlen = jnp.where(grp == 0, 2, jnp.where(grp == 1, 4, jnp.where(grp == 2, 8, 16)))
    pos = pos0 - hist + _iota((n_ext, 1), 0)
    cnt = jnp.minimum(wlen, pos + 1).astype(F32)
    diff = (wsum / cnt - u)[hist:, :]
    o_ref[0] = (_dot(diff.astype(BF16), w_ref[...]) * sc_ref[...]).astype(BF16)


def _pool(u_ext, w_bd, scale, hist, pos0):
    b, n_ext, _ = u_ext.shape
    n_out = n_ext - hist
    return pl.pallas_call(
        functools.partial(_pool_kernel, hist=hist, pos0=pos0),
        grid=(b,),
        in_specs=[pl.BlockSpec((1, n_ext, GROUP_W), lambda i: (i, 0, 0)),
                  _const_spec((GROUP_W, GROUP_W)), _const_spec((1, GROUP_W))],
        out_specs=pl.BlockSpec((1, n_out, GROUP_W), lambda i: (i, 0, 0)),
        out_shape=jax.ShapeDtypeStruct((b, n_out, GROUP_W), BF16),
        scratch_shapes=[pltpu.VMEM((max(POOL_WINDOWS) + n_ext, GROUP_W), F32)],
        compiler_params=_params("parallel"),
        name="pool",
    )(u_ext, w_bd, scale)


def _pool_block_diag(w):
    z = jnp.zeros((GROUP_W, GROUP_W), F32)
    for gi in range(len(POOL_WINDOWS)):
        z = z.at[gi * POOL_GW:(gi + 1) * POOL_GW, gi * POOL_GW:(gi + 1) * POOL_GW].set(w[gi])
    return z.astype(BF16)


QPAD = SUBLANES


def _pad_rows(x, n):
    return jnp.concatenate([x, jnp.zeros((n - x.shape[0], x.shape[1]), x.dtype)], axis=0)


def _block_diag_q(q, groups, width):
    qp = _pad_rows(q, QPAD)
    qq = jnp.concatenate([qp] * groups, axis=0)
    shape = qq.shape
    keep = (_iota(shape, 0) >> 3) == (_iota(shape, 1) // width)
    return jnp.where(keep, qq, jnp.zeros_like(qq))


def _nsa_sample_kernel(pt_ref, q_ref, g_ref, rn_ref, wn_ref, ws_ref, *rest, past, npg, tq, page):
    pages = rest[:npg]
    o_ref = rest[npg]
    s_ref = rest[npg + 1]
    del pt_ref
    scale = HEAD_DIM ** -0.5
    rows = N_HEADS * QPAD
    nc = past // CMP_BLOCK
    ns = past // SEL_BLOCK + 1
    cpp = page // CMP_BLOCK
    spp = page // SEL_BLOCK

    qp = _pad_rows(q_ref[0], QPAD)
    q4 = jnp.concatenate([qp[:, HEAD_DIM * h:HEAD_DIM * (h + 1)] for h in range(N_HEADS)], axis=0)
    iq = _iota((rows, 1), 0) & (QPAD - 1)
    qpos = past + iq

    r = _iota((nc, page), 0)
    t = _iota((nc, page), 1)
    kcvc = jnp.zeros((nc, LANES), F32)
    for p in range(npg):
        pm = jnp.where((t >> 5) + cpp * p == r, 1.0 / CMP_BLOCK, 0.0).astype(BF16)
        kcvc = kcvc + _dot_split_rhs(pm, pages[p][0, 0, :, 0:LANES], 2)
    kc = kcvc[:, 0:HEAD_DIM].astype(BF16)
    vc = kcvc[:, HEAD_DIM:2 * HEAD_DIM].astype(BF16)
    cend = (_iota((1, nc), 1) + 1) * CMP_BLOCK - 1
    e, l = _softmax_parts(_dot_nt(q4, kc) * scale, cend <= qpos)
    pc = e / l
    oc = _dot(pc.astype(BF16), vc)
    pcs = pc[0:QPAD]
    for h in range(1, N_HEADS):
        pcs = pcs + pc[h * QPAD:(h + 1) * QPAD]
    pcs = jnp.concatenate([pcs] * N_HEADS, axis=0)
    pair = jnp.where((_iota((nc, LANES), 0) >> 1) == _iota((nc, LANES), 1), 1.0, 0.0).astype(BF16)
    imp = _dot_split_lhs(pcs, pair, 3)
    sel = _select_blocks(imp, qpos, ns)

    lane = _iota((1, page), 1)
    m = jnp.full((rows, 1), NEG_INF, F32)
    for p in range(npg):
        s = _dot_nt(q4, pages[p][0, 0, :, 128:192].astype(BF16)) * scale
        selk = sel[:, spp * p:spp * p + 1]
        for j in range(1, spp):
            selk = jnp.where(lane >= j * SEL_BLOCK, sel[:, spp * p + j:spp * p + j + 1], selk)
        s = jnp.where(selk > 0.5, s, NEG_INF)
        s_ref[p] = s
        m = jnp.maximum(m, jnp.max(s, axis=-1, keepdims=True))
    rn = _pad_rows(rn_ref[0], QPAD)
    jn = _iota((1, QPAD), 1)
    nmask = (jn <= iq) & (jn < tq) & (sel[:, ns - 1:ns] > 0.5)
    sn = jnp.where(nmask, _dot_nt(q4, rn[:, 128:192].astype(BF16)) * scale, NEG_INF)
    m = jnp.maximum(m, jnp.max(sn, axis=-1, keepdims=True))
    m = jnp.where(m == NEG_INF, 0.0, m)
    en = jnp.exp(sn - m)
    l = jnp.sum(en, axis=-1, keepdims=True)
    acc = _dot(en.astype(BF16), rn[:, 192:256].astype(BF16))
    for p in range(npg):
        e = jnp.exp(s_ref[p] - m)
        l = l + jnp.sum(e, axis=-1, keepdims=True)
        acc = acc + _dot(e.astype(BF16), pages[p][0, 0, :, 192:256].astype(BF16))
    o_s = acc / jnp.maximum(l, 1e-30)

    wb = ws_ref.shape[1]
    ws = ws_ref[0]
    wn = _pad_rows(wn_ref[0], QPAD)
    kwpos = past - wb + _iota((1, wb), 1)
    wmask = (kwpos <= qpos) & (kwpos > qpos - WINDOW) & (kwpos >= 0)
    s1 = jnp.where(wmask, _dot_nt(q4, ws[:, 0:HEAD_DIM].astype(BF16)) * scale, NEG_INF)
    s2 = jnp.where((jn <= iq) & (jn < tq), _dot_nt(q4, wn[:, 0:HEAD_DIM].astype(BF16)) * scale, NEG_INF)
    m = jnp.maximum(jnp.max(s1, axis=-1, keepdims=True), jnp.max(s2, axis=-1, keepdims=True))
    m = jnp.where(m == NEG_INF, 0.0, m)
    e1 = jnp.exp(s1 - m)
    e2 = jnp.exp(s2 - m)
    l = jnp.sum(e1, axis=-1, keepdims=True) + jnp.sum(e2, axis=-1, keepdims=True)
    o_w = (_dot(e1.astype(BF16), ws[:, HEAD_DIM:2 * HEAD_DIM].astype(BF16))
           + _dot(e2.astype(BF16), wn[:, HEAD_DIM:2 * HEAD_DIM].astype(BF16))) / jnp.maximum(l, 1e-30)

    g = _pad_rows(g_ref[0], QPAD)
    outs = []
    for h in range(N_HEADS):
        rs = slice(h * QPAD, (h + 1) * QPAD)
        outs.append(g[:, 3 * h:3 * h + 1] * oc[rs] + g[:, 3 * h + 1:3 * h + 2] * o_s[rs]
                    + g[:, 3 * h + 2:3 * h + 3] * o_w[rs])
    o_ref[0] = jnp.concatenate(outs, axis=1)[0:tq].astype(BF16)


def _page_specs(layer, npg, page, width):
    return [pl.BlockSpec((1, 1, page, width), lambda b, pt, j=j: (layer, pt[b, j], 0, 0))
            for j in range(npg)]


def _nsa_sample(page_table, q, g, rows_new, win_new, win_state, cache, layer):
    db, tq, _ = q.shape
    npg = page_table.shape[1]
    page = cache.shape[2]
    past = npg * page
    wb = win_state.shape[2]
    tok = lambda c: pl.BlockSpec((1, tq, c), lambda b, pt: (b, 0, 0))
    grid_spec = pltpu.PrefetchScalarGridSpec(
        num_scalar_prefetch=1,
        grid=(db,),
        in_specs=[tok(256), tok(128), tok(256), tok(128),
                  pl.BlockSpec((1, 1, wb, 128), lambda b, pt: (layer, b, 0, 0))]
        + _page_specs(layer, npg, page, 256),
        out_specs=tok(256),
        scratch_shapes=[pltpu.VMEM((npg, N_HEADS * QPAD, page), F32)],
    )
    kern = functools.partial(_nsa_sample_kernel, past=past, npg=npg, tq=tq, page=page)

    def wrapped(pt_ref, q_ref, g_ref, rn_ref, wn_ref, ws_ref, *rest):
        kern(pt_ref, q_ref, g_ref, rn_ref, wn_ref, ws_ref.at[0], *rest)

    return pl.pallas_call(
        wrapped,
        grid_spec=grid_spec,
        out_shape=jax.ShapeDtypeStruct((db, tq, GROUP_W), BF16),
        compiler_params=_params("arbitrary"),
        name="nsa_sample",
    )(page_table, q, g, rows_new, win_new, win_state, *([cache] * npg))


def _sb_sample_kernel(pt_ref, q_ref, kvn_ref, *rest, past, npg, tq, page):
    pages = rest[:npg]
    o_ref = rest[npg]
    l1m_ref, lsig_ref = rest[npg + 1], rest[npg + 2]
    del pt_ref
    scale = HEAD_DIM ** -0.5
    rows = N_HEADS * QPAD
    qbd = _block_diag_q(q_ref[0], N_HEADS, HEAD_DIM)
    iq = _iota((rows, 1), 0) & (QPAD - 1)
    later = jnp.where(_iota((page, page), 0) > _iota((page, page), 1), 1.0, 0.0).astype(BF16)

    for p in range(npg):
        z = _dot_nt(qbd, pages[p][0, 0, :, 0:GROUP_W].astype(BF16)) * scale
        sp = _softplus(z)
        l1m_ref[p] = -sp
        lsig_ref[p] = z - sp

    kvn = _pad_rows(kvn_ref[0], QPAD)
    jn = _iota((1, QPAD), 1)
    nmask = (jn < iq) & (jn < tq)
    zn = _dot_nt(qbd, kvn[:, 0:GROUP_W].astype(BF16)) * scale
    spn = _softplus(zn)
    l1n = jnp.where(nmask, -spn, 0.0)
    later_n = jnp.where(_iota((QPAD, QPAD), 0) > _iota((QPAD, QPAD), 1), 1.0, 0.0).astype(BF16)
    sufn = _dot_split_lhs(l1n, later_n, 2)
    an = jnp.where(nmask, jnp.exp(zn - spn + sufn), 0.0)
    acc = _dot(an.astype(BF16), kvn[:, GROUP_W:2 * GROUP_W].astype(BF16))
    run = jnp.sum(l1n, axis=-1, keepdims=True)

    for p in range(npg - 1, -1, -1):
        l1m = l1m_ref[p]
        suf = _dot_split_lhs(l1m, later, 2)
        a = jnp.exp(lsig_ref[p] + suf + run)
        acc = acc + _dot(a.astype(BF16), pages[p][0, 0, :, GROUP_W:2 * GROUP_W].astype(BF16))
        run = run + (suf[:, 0:1] + l1m[:, 0:1])

    keep = (_iota(acc.shape, 0) >> 3) == (_iota(acc.shape, 1) >> 6)
    acc = jnp.where(keep, acc, 0.0)
    out = acc[0:QPAD]
    for h in range(1, N_HEADS):
        out = out + acc[h * QPAD:(h + 1) * QPAD]
    o_ref[0] = out[0:tq].astype(BF16)


def _sb_sample(page_table, q, kv_new, cache, layer):
    db, tq, _ = q.shape
    npg = page_table.shape[1]
    page = cache.shape[2]
    tok = lambda c: pl.BlockSpec((1, tq, c), lambda b, pt: (b, 0, 0))
    grid_spec = pltpu.PrefetchScalarGridSpec(
        num_scalar_prefetch=1,
        grid=(db,),
        in_specs=[tok(256), tok(512)] + _page_specs(layer, npg, page, 512),
        out_specs=tok(256),
        scratch_shapes=[pltpu.VMEM((npg, N_HEADS * QPAD, page), F32)] * 2,
    )
    return pl.pallas_call(
        functools.partial(_sb_sample_kernel, past=npg * page, npg=npg, tq=tq, page=page),
        grid_spec=grid_spec,
        out_shape=jax.ShapeDtypeStruct((db, tq, GROUP_W), BF16),
        compiler_params=_params("arbitrary"),
        name="sb_sample",
    )(page_table, q, kv_new, *([cache] * npg))


def _diff_sample_kernel(pt_ref, q_ref, kvn_ref, lam_ref, gn_ref, *rest, past, npg, tq, page, lam_init):
    pages = rest[:npg]
    o_ref = rest[npg]
    s_ref = rest[npg + 1]
    del pt_ref
    scale = DIFF_DQ ** -0.5
    groups = 2 * N_HEADS
    rows = groups * QPAD
    qbd = _block_diag_q(q_ref[0], groups, DIFF_DQ)
    iq = _iota((rows, 1), 0) & (QPAD - 1)

    m = jnp.full((rows, 1), NEG_INF, F32)
    for p in range(npg):
        s = _dot_nt(qbd, pages[p][0, 0, :, 0:GROUP_W].astype(BF16)) * scale
        s_ref[p] = s
        m = jnp.maximum(m, jnp.max(s, axis=-1, keepdims=True))
    kvn = _pad_rows(kvn_ref[0], QPAD)
    jn = _iota((1, QPAD), 1)
    nmask = (jn <= iq) & (jn < tq)
    sn = jnp.where(nmask, _dot_nt(qbd, kvn[:, 0:GROUP_W].astype(BF16)) * scale, NEG_INF)
    m = jnp.maximum(m, jnp.max(sn, axis=-1, keepdims=True))
    en = jnp.exp(sn - m)
    l = jnp.sum(en, axis=-1, keepdims=True)
    acc = _dot(en.astype(BF16), kvn[:, GROUP_W:2 * GROUP_W].astype(BF16))
    for p in range(npg):
        e = jnp.exp(s_ref[p] - m)
        l = l + jnp.sum(e, axis=-1, keepdims=True)
        acc = acc + _dot(e.astype(BF16), pages[p][0, 0, :, GROUP_W:2 * GROUP_W].astype(BF16))
    o = acc / jnp.maximum(l, 1e-30)

    lam = _diff_lambda(lam_ref, lam_init)
    gn = gn_ref[...]
    outs = []
    for h in range(N_HEADS):
        cs = slice(HEAD_DIM * h, HEAD_DIM * (h + 1))
        o1 = o[(2 * h) * QPAD:(2 * h + 1) * QPAD, cs]
        o2 = o[(2 * h + 1) * QPAD:(2 * h + 2) * QPAD, cs]
        outs.append(_diff_finish(o1, o2, lam, gn, lam_init))
    o_ref[0] = jnp.concatenate(outs, axis=1)[0:tq].astype(BF16)


def _diff_sample(page_table, q, kv_new, lam, gn, cache, layer, lam_init):
    db, tq, _ = q.shape
    npg = page_table.shape[1]
    page = cache.shape[2]
    tok = lambda c: pl.BlockSpec((1, tq, c), lambda b, pt: (b, 0, 0))
    grid_spec = pltpu.PrefetchScalarGridSpec(
        num_scalar_prefetch=1,
        grid=(db,),
        in_specs=[tok(256), tok(512),
                  pl.BlockSpec((4, DIFF_DQ), lambda b, pt: (0, 0)),
                  pl.BlockSpec((1, HEAD_DIM), lambda b, pt: (0, 0))]
        + _page_specs(layer, npg, page, 512),
        out_specs=tok(256),
        scratch_shapes=[pltpu.VMEM((npg, 2 * N_HEADS * QPAD, page), F32)],
    )
    return pl.pallas_call(
        functools.partial(_diff_sample_kernel, past=npg * page, npg=npg, tq=tq, page=page,
                          lam_init=lam_init),
        grid_spec=grid_spec,
        out_shape=jax.ShapeDtypeStruct((db, tq, GROUP_W), BF16),
        compiler_params=_params("arbitrary"),
        name="diff_sample",
    )(page_table, q, kv_new, lam, gn, *([cache] * npg))


def _mix_residual(x_ref, oa_ref, ob_ref, oc_ref, od_ref, wo_ref):
    mixed = jnp.concatenate([oa_ref[...], ob_ref[...], oc_ref[...], od_ref[...]], axis=1)
    return x_ref[...] + _dot(mixed, wo_ref[...])


def _swiglu_act(h, wg_ref, wu_ref, act_ref, width, chunk, lead=()):
    for c0 in range(0, width, chunk):
        c1 = min(c0 + chunk, width)
        idx = lead + (slice(None), slice(c0, c1))
        gate = _dot(h, wg_ref[idx])
        up = _dot(h, wu_ref[idx])
        act_ref[:, c0:c1] = (_silu(gate) * up).astype(BF16)


def _ffn_kernel(x_ref, oa_ref, ob_ref, oc_ref, od_ref, wo_ref, g_ref, wg_ref, wu_ref, wd_ref,
                o_ref, act_ref):
    x = _mix_residual(x_ref, oa_ref, ob_ref, oc_ref, od_ref, wo_ref)
    h = _rmsnorm(x, g_ref[...]).astype(BF16)
    _swiglu_act(h, wg_ref, wu_ref, act_ref, wg_ref.shape[1], 512)
    o_ref[...] = x + _dot(act_ref[...], wd_ref[...])


def _ffn(x, o4, wo, g, wg, wu, wd, tm):
    n = x.shape[0]
    f = wg.shape[1]
    row = lambda c: pl.BlockSpec((tm, c), lambda i: (i, 0))
    return pl.pallas_call(
        _ffn_kernel,
        grid=(n // tm,),
        in_specs=[row(D_MODEL)] + [row(GROUP_W)] * 4
        + [_const_spec((D_MODEL, D_MODEL)), _const_spec((1, D_MODEL)),
           _const_spec((D_MODEL, f)), _const_spec((D_MODEL, f)), _const_spec((f, D_MODEL))],
        out_specs=row(D_MODEL),
        out_shape=jax.ShapeDtypeStruct((n, D_MODEL), F32),
        scratch_shapes=[pltpu.VMEM((tm, f), BF16)],
        compiler_params=_params("parallel"),
        name="ffn",
    )(x, *o4, wo, g, wg, wu, wd)


def _moe_kernel(x_ref, oa_ref, ob_ref, oc_ref, od_ref, wo_ref, g_ref, wr_ref, wg_ref, wu_ref, wd_ref,
                o_ref, xs_ref, h_ref, gate_ref, acc_ref, act_ref):
    e = pl.program_id(1)
    lane = _iota((1, LANES), 1)

    @pl.when(e == 0)
    def _():
        x = _mix_residual(x_ref, oa_ref, ob_ref, oc_ref, od_ref, wo_ref)
        xs_ref[...] = x
        h = _rmsnorm(x, g_ref[...]).astype(BF16)
        h_ref[...] = h
        lanef = lane.astype(F32)
        logits = jnp.where(lane < N_EXPERTS, _dot(h, wr_ref[...]), NEG_INF)
        v1 = jnp.max(logits, axis=-1, keepdims=True)
        i1 = jnp.min(jnp.where(logits == v1, lanef, float(LANES)), axis=-1, keepdims=True)
        rest = jnp.where(lanef == i1, NEG_INF, logits)
        v2 = jnp.max(rest, axis=-1, keepdims=True)
        i2 = jnp.min(jnp.where(rest == v2, lanef, float(LANES)), axis=-1, keepdims=True)
        t = jnp.exp(v2 - v1)
        gate_ref[...] = (jnp.where(lanef == i1, 1.0 / (1.0 + t), 0.0)
                         + jnp.where(lanef == i2, t / (1.0 + t), 0.0))
        acc_ref[...] = jnp.zeros(acc_ref.shape, F32)

    gcol = jnp.sum(jnp.where(lane == e, gate_ref[...], 0.0), axis=-1, keepdims=True)
    _swiglu_act(h_ref[...], wg_ref, wu_ref, act_ref, wg_ref.shape[2], 512, lead=(0,))
    acc_ref[...] += gcol * _dot(act_ref[...], wd_ref[0])

    @pl.when(e == pl.num_programs(1) - 1)
    def _():
        o_ref[...] = xs_ref[...] + acc_ref[...]


def _moe(x, o4, wo, g, wr, wg, wu, wd, tm):
    n = x.shape[0]
    ne, _, f = wg.shape
    row = lambda c: pl.BlockSpec((tm, c), lambda i, e: (i, 0))
    const = lambda shape: pl.BlockSpec(shape, lambda i, e: (0,) * len(shape))
    return pl.pallas_call(
        _moe_kernel,
        grid=(n // tm, ne),
        in_specs=[row(D_MODEL)] + [row(GROUP_W)] * 4
        + [const((D_MODEL, D_MODEL)), const((1, D_MODEL)), const((D_MODEL, LANES)),
           pl.BlockSpec((1, D_MODEL, f), lambda i, e: (e, 0, 0)),
           pl.BlockSpec((1, D_MODEL, f), lambda i, e: (e, 0, 0)),
           pl.BlockSpec((1, f, D_MODEL), lambda i, e: (e, 0, 0))],
        out_specs=row(D_MODEL),
        out_shape=jax.ShapeDtypeStruct((n, D_MODEL), F32),
        scratch_shapes=[pltpu.VMEM((tm, D_MODEL), F32), pltpu.VMEM((tm, D_MODEL), BF16),
                        pltpu.VMEM((tm, LANES), F32), pltpu.VMEM((tm, D_MODEL), F32),
                        pltpu.VMEM((tm, f), BF16)],
        compiler_params=_params("parallel", "arbitrary"),
        name="moe",
    )(x, *o4, wo, g, wr, wg, wu, wd)


def _final_norm_kernel(x_ref, g_ref, o_ref):
    o_ref[...] = _rmsnorm(x_ref[...], g_ref[...])


def _final_norm(x, g, tm):
    n = x.shape[0]
    row = pl.BlockSpec((tm, D_MODEL), lambda i: (i, 0))
    return pl.pallas_call(
        _final_norm_kernel,
        grid=(n // tm,),
        in_specs=[row, _const_spec((1, D_MODEL))],
        out_specs=row,
        out_shape=jax.ShapeDtypeStruct((n, D_MODEL), F32),
        compiler_params=_params("parallel"),
        name="final_norm",
    )(x, g)


def _row_tile(n, cap):
    t = min(n, cap)
    while n % t:
        t //= 2
    return t


@jax.jit
def kernel(x_prompt, x_sample, cache_nsa, cache_sb, cache_diff, state_nsa_win, state_pool, page_table,
           norm_mix, w_in, w_out, diff_lambda, diff_norm, pool_w, pool_scale, norm_ffn,
           ffn_gate, ffn_up, ffn_down, router_w, moe_gate, moe_up, moe_down, norm_final):
    b, s, d = x_prompt.shape
    db, t, _ = x_sample.shape
    depth, n_pool, page = cache_nsa.shape[:3]
    past = page_table.shape[1] * page
    wb = state_nsa_win.shape[2]
    wkeep = min(WINDOW, s)
    np_, ns_ = b * s, db * t
    tm_p = _row_tile(s, 512)
    tm_s = _row_tile(ns_, 512)

    tabs_p = _rope_tables(jnp.arange(s))
    tabs_s = _rope_tables(past + (jnp.arange(tm_s) % t))
    cache_nsa = cache_nsa.reshape(depth, n_pool, page, 4 * HEAD_DIM)
    cache_sb = cache_sb.reshape(depth, n_pool, page, 2 * GROUP_W)
    cache_diff = cache_diff.reshape(depth, n_pool, page, 2 * GROUP_W)
    state_win = state_nsa_win.reshape(depth, db, wb, 2 * HEAD_DIM)

    xp = x_prompt.reshape(np_, d)
    xs = x_sample.reshape(ns_, d)
    outs = [[] for _ in range(10)]
    for l in range(depth):
        lam_init = 0.8 - 0.6 * math.exp(-0.3 * l)
        g_mix = norm_mix[l].reshape(1, d)
        w_l = _prep_w_in(w_in[l])
        lam_l = diff_lambda[l]
        gn_l = diff_norm[l].reshape(1, HEAD_DIM)
        wp_l = _pool_block_diag(pool_w[l])
        ps_l = pool_scale[l].reshape(1, GROUP_W)

        qa, rows, win, gate, qb, kvb, qc, kvc, du = _project(xp, g_mix, w_l, tabs_p, tm_p)
        qa_s, rows_s, win_s, gate_s, qb_s, kvb_s, qc_s, kvc_s, du_s = _project(xs, g_mix, w_l, tabs_s, tm_s)

        p3 = lambda a: a.reshape(b, s, a.shape[-1])
        s3 = lambda a: a.reshape(db, t, a.shape[-1])
        oa = _nsa_prompt(p3(qa), p3(gate), p3(rows), p3(win))
        ob = _sb_prompt(p3(qb), p3(kvb))
        oc = _diff_prompt(p3(qc), p3(kvc), lam_l, gn_l, lam_init)
        od = _pool(p3(du), wp_l, ps_l, 0, 0)

        oa_s = _nsa_sample(page_table, s3(qa_s), s3(gate_s), s3(rows_s), s3(win_s), state_win, cache_nsa, l)
        ob_s = _sb_sample(page_table, s3(qb_s), s3(kvb_s), cache_sb, l)
        oc_s = _diff_sample(page_table, s3(qc_s), s3(kvc_s), lam_l, gn_l, cache_diff, l, lam_init)
        u_ext = jnp.concatenate([state_pool[l], s3(du_s)], axis=1)
        od_s = _pool(u_ext, wp_l, ps_l, POOL_HIST, past)

        outs[0].append(rows.reshape(b, s, 4, HEAD_DIM))
        outs[1].append(rows_s.reshape(db, t, 4, HEAD_DIM))
        outs[2].append(kvb.reshape(b, s, 2, N_HEADS, HEAD_DIM))
        outs[3].append(kvb_s.reshape(db, t, 2, N_HEADS, HEAD_DIM))
        outs[4].append(kvc.reshape(b, s, 2, N_HEADS, HEAD_DIM))
        outs[5].append(kvc_s.reshape(db, t, 2, N_HEADS, HEAD_DIM))
        outs[6].append(win.reshape(b, s, 2, HEAD_DIM)[:, s - wkeep:])
        outs[7].append(jnp.concatenate([state_nsa_win[l], win_s.reshape(db, t, 2, HEAD_DIM)], axis=1)[:, -wb:])
        outs[8].append(p3(du)[:, s - POOL_HIST:])
        outs[9].append(u_ext[:, -POOL_HIST:])

        o4_p = [a.reshape(np_, GROUP_W) for a in (oa, ob, oc, od)]
        o4_s = [a.reshape(ns_, GROUP_W) for a in (oa_s, ob_s, oc_s, od_s)]
        wo = w_out[l].astype(BF16)
        g_ffn = norm_ffn[l].reshape(1, d)
        i = l // 2
        if l % 2 == 0:
            wg, wu, wd = ffn_gate[i].astype(BF16), ffn_up[i].astype(BF16), ffn_down[i].astype(BF16)
            xp = _ffn(xp, o4_p, wo, g_ffn, wg, wu, wd, tm_p)
            xs = _ffn(xs, o4_s, wo, g_ffn, wg, wu, wd, tm_s)
        else:
            wr = jnp.pad(router_w[i], ((0, 0), (0, LANES - N_EXPERTS))).astype(BF16)
            wg, wu, wd = moe_gate[i].astype(BF16), moe_up[i].astype(BF16), moe_down[i].astype(BF16)
            xp = _moe(xp, o4_p, wo, g_ffn, wr, wg, wu, wd, tm_p)
            xs = _moe(xs, o4_s, wo, g_ffn, wr, wg, wu, wd, tm_s)

    g_fin = norm_final.reshape(1, d)
    y_p = _final_norm(xp, g_fin, tm_p).reshape(b, s, d)
    y_s = _final_norm(xs, g_fin, tm_s).reshape(db, t, d)
    return (y_p, y_s) + tuple(jnp.stack(o) for o in outs)
```

```python
import functools
import math

import numpy as np
import jax
import jax.numpy as jnp
from jax import lax
from jax.experimental import pallas as pl
from jax.experimental.pallas import tpu as pltpu

F32 = jnp.float32
BF16 = jnp.bfloat16
I32 = jnp.int32

D_MODEL = 1024
HEAD_DIM = 64
GROUP_W = D_MODEL // 4
N_HEADS = GROUP_W // HEAD_DIM
DIFF_DQ = HEAD_DIM // 2
ROPE_THETA = 500000.0
CMP_BLOCK = 32
SEL_BLOCK = 64
SEL_TOPK = 8
WINDOW = 512
FORCE_SCORE = 1.0e4
POOL_WINDOWS = (2, 4, 8, 16)
POOL_GW = GROUP_W // len(POOL_WINDOWS)
POOL_HIST = max(POOL_WINDOWS) - 1
N_EXPERTS = 8
NORM_EPS = 1e-6
PROJ_SIZES = (256, 64, 64, 64, 64, 64, 64, 12, 256, 256, 256, 256, 256, 256, 256)

QBLOCK = 128
SB_BLOCK = 256
N_EXTENTS = 4
LANES = 128
SUBLANES = 8
VMEM_LIMIT = 56 * 1024 * 1024

C_AQ, C_ROWS, C_WIN, C_GATE, C_BQ, C_BKV, C_CQ, C_CKV, C_DU, C_END = (
    0, 256, 512, 640, 768, 1024, 1536, 1792, 2304, 2560)

NEG_INF = float("-inf")
QK_SCALE = HEAD_DIM ** -0.5
DIFF_SCALE = DIFF_DQ ** -0.5
EXP_UNDERFLOW = -104.0


def _dot(a, b):
    return jnp.dot(a, b, preferred_element_type=F32)


def _dot_nt(a, b):
    return lax.dot_general(a, b, (((1,), (1,)), ((), ())), preferred_element_type=F32)


def _split_bf16(x, parts):
    out = []
    r = x
    for _ in range(parts):
        p = r.astype(BF16)
        out.append(p)
        r = r - p.astype(F32)
    return out


def _dot_split_lhs(x, w, parts):
    acc = None
    for p in _split_bf16(x, parts):
        t = _dot(p, w)
        acc = t if acc is None else acc + t
    return acc


def _dot_split_rhs(w, x, parts):
    acc = None
    for p in _split_bf16(x, parts):
        t = _dot(w, p)
        acc = t if acc is None else acc + t
    return acc


def _iota(shape, dim):
    return lax.broadcasted_iota(I32, shape, dim)


def _later_keys(n):
    return jnp.where(_iota((n, n), 0) > _iota((n, n), 1), 1.0, 0.0).astype(BF16)


def _rmsnorm(x, g):
    ms = jnp.mean(x * x, axis=-1, keepdims=True)
    return x * lax.rsqrt(ms + NORM_EPS) * g


def _softmax_parts(s, mask):
    s = jnp.where(mask, s, NEG_INF)
    m = jnp.max(s, axis=-1, keepdims=True)
    m = jnp.where(m == NEG_INF, 0.0, m)
    e = jnp.where(mask, jnp.exp(s - m), 0.0)
    l = jnp.maximum(jnp.sum(e, axis=-1, keepdims=True), 1e-30)
    return e, l


def _softplus(z):
    return jnp.maximum(z, 0.0) + jnp.log1p(jnp.exp(-jnp.abs(z)))


def _silu(x):
    return x / (1.0 + jnp.exp(-x))


def _params(*sem):
    return pltpu.CompilerParams(dimension_semantics=sem, vmem_limit_bytes=VMEM_LIMIT)


def _const_spec(shape):
    nd = len(shape)
    return pl.BlockSpec(shape, lambda *_: (0,) * nd)


def _rope_tile(f, c, s1, s2, shift):
    return f * c + pltpu.roll(f, shift, 1) * s1 + pltpu.roll(f, LANES - shift, 1) * s2


def _proj_kernel(x_ref, g_ref, w_ref, ca_ref, sa1_ref, sa2_ref, cc_ref, sc1_ref, sc2_ref,
                 qa_ref, rows_ref, win_ref, gate_ref, qb_ref, kvb_ref, qc_ref, kvc_ref, du_ref):
    h = _rmsnorm(x_ref[...], g_ref[...]).astype(BF16)

    def mm(c0, c1):
        return _dot(h, w_ref[:, c0:c1])

    ca, sa1, sa2 = ca_ref[...], sa1_ref[...], sa2_ref[...]
    cc, sc1, sc2 = cc_ref[...], sc1_ref[...], sc2_ref[...]
    first = _iota(ca.shape, 1) < HEAD_DIM
    cb = jnp.where(first, ca, 1.0)
    sb1 = jnp.where(first, sa1, 0.0)
    sb2 = jnp.where(first, sa2, 0.0)
    ha = HEAD_DIM // 8
    hc = DIFF_DQ // 8

    for t in range(2):
        f = mm(C_AQ + t * LANES, C_AQ + (t + 1) * LANES)
        qa_ref[:, t * LANES:(t + 1) * LANES] = (_rope_tile(f, ca, sa1, sa2, ha) * QK_SCALE).astype(BF16)
    for t in range(2):
        f = mm(C_ROWS + t * LANES, C_ROWS + (t + 1) * LANES)
        rows_ref[:, t * LANES:(t + 1) * LANES] = _rope_tile(f, cb, sb1, sb2, ha)
    win_ref[...] = _rope_tile(mm(C_WIN, C_GATE), cb, sb1, sb2, ha)
    gate_ref[...] = 1.0 / (1.0 + jnp.exp(-mm(C_GATE, C_BQ)))
    qb_ref[...] = (mm(C_BQ, C_BKV) * QK_SCALE).astype(BF16)
    kvb_ref[...] = mm(C_BKV, C_CQ)
    for t in range(2):
        f = mm(C_CQ + t * LANES, C_CQ + (t + 1) * LANES)
        qc_ref[:, t * LANES:(t + 1) * LANES] = (_rope_tile(f, cc, sc1, sc2, hc) * DIFF_SCALE).astype(BF16)
    for t in range(2):
        f = mm(C_CKV + t * LANES, C_CKV + (t + 1) * LANES)
        kvc_ref[:, t * LANES:(t + 1) * LANES] = _rope_tile(f, cc, sc1, sc2, hc)
    kvc_ref[:, 2 * LANES:4 * LANES] = mm(C_CKV + 2 * LANES, C_DU)
    du_ref[...] = mm(C_DU, C_END)


def _project(x, g, w, tabs, tm):
    n = x.shape[0]
    p = tabs[0].shape[0]
    nper = p // tm
    row = lambda c: pl.BlockSpec((tm, c), lambda i: (i, 0))
    tab = pl.BlockSpec((tm, LANES), lambda i: (i % nper, 0))
    outs = [(256, BF16), (256, F32), (128, F32), (128, F32), (256, BF16), (512, F32),
            (256, BF16), (512, F32), (256, F32)]
    return pl.pallas_call(
        _proj_kernel,
        grid=(n // tm,),
        in_specs=[row(D_MODEL), _const_spec((1, D_MODEL)), _const_spec((D_MODEL, C_END))] + [tab] * 6,
        out_specs=[row(c) for c, _ in outs],
        out_shape=[jax.ShapeDtypeStruct((n, c), d) for c, d in outs],
        compiler_params=_params("parallel"),
        name="proj",
    )(x, g, w, *tabs)


def _rope_tables(pos):
    pos = pos.astype(F32)[:, None]
    lane = np.arange(LANES)
    out = []
    for width in (HEAD_DIM, DIFF_DQ):
        half = width // 8
        d = lane % width
        inv = ROPE_THETA ** (-jnp.arange(half, dtype=F32) / half)
        ang = pos * inv[None, :]
        cos, sin = jnp.cos(ang), jnp.sin(ang)
        is1 = d < half
        is2 = (d >= half) & (d < 2 * half)
        fidx = np.where(is1, d, np.where(is2, d - half, 0))
        cosl, sinl = cos[:, fidx], sin[:, fidx]
        rot = jnp.asarray(is1 | is2)[None, :]
        out.append(jnp.where(rot, cosl, 1.0))
        out.append(jnp.where(jnp.asarray(is2)[None, :], sinl, 0.0))
        out.append(jnp.where(jnp.asarray(is1)[None, :], -sinl, 0.0))
    return out


def _prep_w_in(w):
    cuts = np.cumsum(PROJ_SIZES)[:-1].tolist()
    (aq, akc, avc, aks, avs, akw, avw, ag, bq, bk, bv, cq, ck, cv, du) = jnp.split(w, cuts, axis=-1)
    agp = jnp.pad(ag, ((0, 0), (0, LANES - ag.shape[1])))
    return jnp.concatenate([aq, akc, avc, aks, avs, akw, avw, agp, bq, bk, bv, cq, ck, cv, du],
                           axis=-1).astype(BF16)


def _select_blocks(imp, qpos, ns):
    w = imp.shape[1]
    blk = _iota((1, w), 1)
    cur = qpos >> 6
    forced = (blk == 0) | (blk == cur) | (blk == cur - 1)
    valid = (blk * SEL_BLOCK <= qpos) & (blk < ns)
    imp = jnp.where(forced, FORCE_SCORE, imp)
    imp = jnp.where(valid, imp, NEG_INF)
    rank = jnp.zeros(imp.shape, F32)
    for i in range(ns):
        ci = imp[:, i:i + 1]
        tie = jnp.where(blk > i, 1.0, 0.0)
        rank = rank + jnp.where(ci > imp, 1.0, jnp.where(ci == imp, tie, 0.0))
    return jnp.where((rank < min(SEL_TOPK, ns)) & (blk < ns), 1.0, 0.0)


def _biased_softmax_pv(s, bias, v, heads, rows):
    k = s.shape[1]
    s3 = s.reshape(heads, rows, k) + bias[None]
    m = jnp.max(s3, axis=-1, keepdims=True)
    e = jnp.exp(s3 - m)
    l = jnp.sum(e, axis=-1, keepdims=True)
    o = _dot(e.reshape(heads * rows, k).astype(BF16), v)
    return o / l.reshape(heads * rows, 1)


def _nsa_prompt_kernel(q_ref, g_ref, rows_ref, win_ref, o_ref,
                       kcvc_ref, ks_ref, vs_ref, kw_ref, vw_ref, os_ref, *, seq):
    qi = pl.program_id(1)
    s0 = qi * QBLOCK
    nc = seq // CMP_BLOCK
    ns = seq // SEL_BLOCK
    per = (seq // QBLOCK) // N_EXTENTS

    @pl.when(qi == 0)
    def _():
        r = _iota((nc, seq), 0)
        t = _iota((nc, seq), 1)
        pm = jnp.where((t >> 5) == r, 1.0 / CMP_BLOCK, 0.0).astype(BF16)
        kcvc_ref[...] = _dot_split_rhs(pm, rows_ref[0, :, 0:LANES], 2)
        ks_ref[...] = rows_ref[0, :, 128:192].astype(BF16)
        vs_ref[...] = rows_ref[0, :, 192:256].astype(BF16)
        kw_ref[...] = win_ref[0, :, 0:64].astype(BF16)
        vw_ref[...] = win_ref[0, :, 64:128].astype(BF16)

    q = q_ref[0]
    g = g_ref[0]
    qpos = s0 + _iota((QBLOCK, 1), 0)

    q4 = jnp.concatenate([q[:, HEAD_DIM * h:HEAD_DIM * (h + 1)] for h in range(N_HEADS)], axis=0)
    qpos4 = jnp.concatenate([qpos] * N_HEADS, axis=0)
    kc = kcvc_ref[:, 0:HEAD_DIM].astype(BF16)
    vc = kcvc_ref[:, HEAD_DIM:2 * HEAD_DIM].astype(BF16)
    cend = (_iota((1, nc), 1) + 1) * CMP_BLOCK - 1
    e, l = _softmax_parts(_dot_nt(q4, kc), cend <= qpos4)
    pc = e / l
    oc = _dot(pc.astype(BF16), vc)
    pcs = pc[0:QBLOCK]
    for h in range(1, N_HEADS):
        pcs = pcs + pc[h * QBLOCK:(h + 1) * QBLOCK]
    pair = jnp.where((_iota((nc, ns), 0) >> 1) == _iota((nc, ns), 1), 1.0, 0.0).astype(BF16)
    imp = _dot_split_lhs(pcs, pair, 3)
    sel = _select_blocks(imp, qpos, ns)

    for j in range(N_EXTENTS):
        ext = (j + 1) * per * QBLOCK

        @pl.when(qi // per == j)
        def _(ext=ext):
            nse = ext // SEL_BLOCK
            expand = jnp.where(_iota((nse, ext), 0) == (_iota((nse, ext), 1) >> 6), 1.0, 0.0).astype(BF16)
            selk = _dot(sel[:, 0:nse].astype(BF16), expand)
            live = (selk > 0.5) & (_iota((1, ext), 1) <= qpos)
            bias = jnp.where(live, 0.0, NEG_INF)
            s = _dot_nt(q4, ks_ref[0:ext, :])
            os_ref[...] = _biased_softmax_pv(s, bias, vs_ref[0:ext, :], N_HEADS, QBLOCK)

    wl = WINDOW + QBLOCK
    start = pl.multiple_of(jnp.maximum(s0 - WINDOW, 0), QBLOCK)
    kwpos = start + _iota((1, wl), 1)
    wbias = jnp.where((kwpos <= qpos) & (kwpos > qpos - WINDOW), 0.0, NEG_INF)
    sw = _dot_nt(q4, kw_ref[pl.ds(start, wl), :])
    ow = _biased_softmax_pv(sw, wbias, vw_ref[pl.ds(start, wl), :], N_HEADS, QBLOCK)

    outs = []
    for h in range(N_HEADS):
        rs = slice(h * QBLOCK, (h + 1) * QBLOCK)
        outs.append(g[:, 3 * h:3 * h + 1] * oc[rs] + g[:, 3 * h + 1:3 * h + 2] * os_ref[rs, :]
                    + g[:, 3 * h + 2:3 * h + 3] * ow[rs])
    o_ref[0] = jnp.concatenate(outs, axis=1).astype(BF16)


def _nsa_prompt(q, g, rows, win):
    b, s, _ = q.shape
    qspec = lambda c: pl.BlockSpec((1, QBLOCK, c), lambda i, j: (i, j, 0))
    full = lambda c: pl.BlockSpec((1, s, c), lambda i, j: (i, 0, 0))
    return pl.pallas_call(
        functools.partial(_nsa_prompt_kernel, seq=s),
        grid=(b, s // QBLOCK),
        in_specs=[qspec(256), qspec(128), full(256), full(128)],
        out_specs=qspec(256),
        out_shape=jax.ShapeDtypeStruct((b, s, GROUP_W), BF16),
        scratch_shapes=[pltpu.VMEM((s // CMP_BLOCK, LANES), F32)]
        + [pltpu.VMEM((s, HEAD_DIM), BF16)] * 4
        + [pltpu.VMEM((N_HEADS * QBLOCK, HEAD_DIM), F32)],
        compiler_params=_params("parallel", "arbitrary"),
        name="nsa_prompt",
    )(q, g, rows, win)


def _sb_prompt_kernel(q_ref, kv_ref, o_ref, kb_ref, vb_ref, r_ref, acc_ref):
    qi = pl.program_id(1)
    s0 = qi * SB_BLOCK

    @pl.when(qi == 0)
    def _():
        kb_ref[...] = kv_ref[0, :, 0:GROUP_W].astype(BF16)
        vb_ref[...] = kv_ref[0, :, GROUP_W:2 * GROUP_W].astype(BF16)

    q = q_ref[0]
    r_ref[...] = jnp.zeros(r_ref.shape, F32)
    acc_ref[...] = jnp.zeros(acc_ref.shape, F32)
    later = _later_keys(SB_BLOCK)
    qpos = s0 + _iota((SB_BLOCK, 1), 0)

    def cond(c):
        it, rmax = c
        return (it <= qi) & (rmax > EXP_UNDERFLOW)

    def body(c):
        it, _ = c
        k0 = pl.multiple_of((qi - it) * SB_BLOCK, SB_BLOCK)
        mask = (k0 + _iota((1, SB_BLOCK), 1)) < qpos
        rm = None
        for h in range(N_HEADS):
            cs = slice(HEAD_DIM * h, HEAD_DIM * (h + 1))
            z = _dot_nt(q[:, cs], kb_ref[pl.ds(k0, SB_BLOCK), cs])
            sp = _softplus(z)
            l1m = jnp.where(mask, -sp, 0.0)
            suf = _dot_split_lhs(l1m, later, 2)
            rh = r_ref[h]
            a = jnp.where(mask, jnp.exp(z - sp + suf + jnp.concatenate([rh, rh], axis=1)), 0.0)
            acc_ref[:, cs] += _dot(a.astype(BF16), vb_ref[pl.ds(k0, SB_BLOCK), cs])
            rn = rh + (suf[:, 0:1] + l1m[:, 0:1])
            r_ref[h] = rn
            rm = rn if rm is None else jnp.maximum(rm, rn)
        return it + 1, jnp.max(rm)

    lax.while_loop(cond, body, (jnp.int32(0), jnp.float32(0.0)))
    o_ref[0] = acc_ref[...].astype(BF16)


def _sb_prompt(q, kv):
    b, s, _ = q.shape
    qspec = pl.BlockSpec((1, SB_BLOCK, GROUP_W), lambda i, j: (i, j, 0))
    return pl.pallas_call(
        _sb_prompt_kernel,
        grid=(b, s // SB_BLOCK),
        in_specs=[qspec, pl.BlockSpec((1, s, 2 * GROUP_W), lambda i, j: (i, 0, 0))],
        out_specs=qspec,
        out_shape=jax.ShapeDtypeStruct((b, s, GROUP_W), BF16),
        scratch_shapes=[pltpu.VMEM((s, GROUP_W), BF16), pltpu.VMEM((s, GROUP_W), BF16),
                        pltpu.VMEM((N_HEADS, SB_BLOCK, LANES), F32), pltpu.VMEM((SB_BLOCK, GROUP_W), F32)],
        compiler_params=_params("parallel", "arbitrary"),
        name="sb_prompt",
    )(q, kv)


def _diff_lambda(lam_ref, lam_init):
    lv = lam_ref[...]
    a = jnp.sum(lv[0:1] * lv[1:2], axis=1, keepdims=True)
    b = jnp.sum(lv[2:3] * lv[3:4], axis=1, keepdims=True)
    return jnp.exp(a) - jnp.exp(b) + lam_init


def _diff_finish(o1, o2, lam, gn, lam_init):
    o = o1 - lam * o2
    return _rmsnorm(o, gn) * (1.0 - lam_init)


def _diff_prompt_kernel(q_ref, kv_ref, lam_ref, gn_ref, o_ref, kb_ref, vb_ref, *, seq, lam_init):
    qi = pl.program_id(1)
    s0 = qi * QBLOCK
    per = (seq // QBLOCK) // N_EXTENTS

    @pl.when(qi == 0)
    def _():
        kb_ref[...] = kv_ref[0, :, 0:GROUP_W].astype(BF16)
        vb_ref[...] = kv_ref[0, :, GROUP_W:2 * GROUP_W].astype(BF16)

    lam = _diff_lambda(lam_ref, lam_init)
    gn = gn_ref[...]
    q = q_ref[0]
    qpos = s0 + _iota((QBLOCK, 1), 0)

    for j in range(N_EXTENTS):
        ext = (j + 1) * per * QBLOCK

        @pl.when(qi // per == j)
        def _(ext=ext):
            bias = jnp.where(_iota((1, ext), 1) <= qpos, 0.0, NEG_INF)
            outs = []
            for h in range(N_HEADS):
                es, ls = [], []
                for m in range(2):
                    cs = slice(HEAD_DIM * h + DIFF_DQ * m, HEAD_DIM * h + DIFF_DQ * (m + 1))
                    s = _dot_nt(q[:, cs], kb_ref[0:ext, cs]) + bias
                    e = jnp.exp(s - jnp.max(s, axis=-1, keepdims=True))
                    es.append(e.astype(BF16))
                    ls.append(jnp.sum(e, axis=-1, keepdims=True))
                o = _dot(jnp.concatenate(es, axis=0), vb_ref[0:ext, HEAD_DIM * h:HEAD_DIM * (h + 1)])
                outs.append(_diff_finish(o[0:QBLOCK] / ls[0], o[QBLOCK:2 * QBLOCK] / ls[1],
                                         lam, gn, lam_init))
            o_ref[0] = jnp.concatenate(outs, axis=1).astype(BF16)


def _diff_prompt(q, kv, lam, gn, lam_init):
    b, s, _ = q.shape
    qspec = pl.BlockSpec((1, QBLOCK, GROUP_W), lambda i, j: (i, j, 0))
    return pl.pallas_call(
        functools.partial(_diff_prompt_kernel, seq=s, lam_init=lam_init),
        grid=(b, s // QBLOCK),
        in_specs=[qspec, pl.BlockSpec((1, s, 2 * GROUP_W), lambda i, j: (i, 0, 0)),
                  pl.BlockSpec((4, DIFF_DQ), lambda i, j: (0, 0)),
                  pl.BlockSpec((1, HEAD_DIM), lambda i, j: (0, 0))],
        out_specs=qspec,
        out_shape=jax.ShapeDtypeStruct((b, s, GROUP_W), BF16),
        scratch_shapes=[pltpu.VMEM((s, GROUP_W), BF16), pltpu.VMEM((s, GROUP_W), BF16)],
        compiler_params=_params("parallel", "arbitrary"),
        name="diff_prompt",
    )(q, kv, lam, gn)


def _pool_kernel(u_ref, w_ref, sc_ref, o_ref, buf_ref, *, hist, pos0):
    n_ext = u_ref.shape[1]
    pad = max(POOL_WINDOWS)
    u = u_ref[0]
    buf_ref[0:pad, :] = jnp.zeros((pad, GROUP_W), F32)
    buf_ref[pad:pad + n_ext, :] = u
    acc = u
    sums = []
    shift = 1
    for _ in POOL_WINDOWS:
        acc = acc + buf_ref[pad - shift:pad - shift + n_ext, :]
        sums.append(acc)
        shift *= 2
        if shift < pad:
            buf_ref[pad:pad + n_ext, :] = acc
    lane = _iota((1, GROUP_W), 1)
    grp = lane >> 6
    wsum = jnp.where(grp == 0, sums[0], jnp.where(grp == 1, sums[1], jnp.where(grp == 2, sums[2], sums[3])))
    wlen = jnp.where(grp == 0, 2, jnp.where(grp == 1, 4, jnp.where(grp == 2, 8, 16)))
    pos = pos0 - hist + _iota((n_ext, 1), 0)
    cnt = jnp.maximum(jnp.minimum(wlen, pos + 1), 1).astype(F32)
    diff = (wsum / cnt - u)[hist:, :]
    o_ref[0] = (_dot(diff.astype(BF16), w_ref[...]) * sc_ref[...]).astype(BF16)


def _pool(u_ext, w_bd, scale, hist, pos0):
    b, n_ext, _ = u_ext.shape
    n_out = n_ext - hist
    return pl.pallas_call(
        functools.partial(_pool_kernel, hist=hist, pos0=pos0),
        grid=(b,),
        in_specs=[pl.BlockSpec((1, n_ext, GROUP_W), lambda i: (i, 0, 0)),
                  _const_spec((GROUP_W, GROUP_W)), _const_spec((1, GROUP_W))],
        out_specs=pl.BlockSpec((1, n_out, GROUP_W), lambda i: (i, 0, 0)),
        out_shape=jax.ShapeDtypeStruct((b, n_out, GROUP_W), BF16),
        scratch_shapes=[pltpu.VMEM((max(POOL_WINDOWS) + n_ext, GROUP_W), F32)],
        compiler_params=_params("parallel"),
        name="pool",
    )(u_ext, w_bd, scale)


def _pool_block_diag(w):
    z = jnp.zeros((GROUP_W, GROUP_W), F32)
    for gi in range(len(POOL_WINDOWS)):
        z = z.at[gi * POOL_GW:(gi + 1) * POOL_GW, gi * POOL_GW:(gi + 1) * POOL_GW].set(w[gi])
    return z.astype(BF16)


QPAD = SUBLANES


def _pad_rows(x, n):
    return jnp.concatenate([x, jnp.zeros((n - x.shape[0], x.shape[1]), x.dtype)], axis=0)


def _block_diag_q(q, groups, width):
    qp = _pad_rows(q, QPAD)
    qq = jnp.concatenate([qp] * groups, axis=0)
    shape = qq.shape
    shift = int(math.log2(width))
    keep = (_iota(shape, 0) >> 3) == (_iota(shape, 1) >> shift)
    return jnp.where(keep, qq, jnp.zeros_like(qq))


SEQ_PER_STEP = 2


def _page_specs(layer, nb, npg, feat, page):
    return [pl.BlockSpec((1, 1, feat, page), lambda b, pt, i=i, j=j: (layer, pt[b * nb + i, j], 0, 0))
            for i in range(nb) for j in range(npg)]


def _per_sequence(body, nb, kinds, npg):
    def kern(pt_ref, *refs):
        n_in = len(kinds)
        pages = refs[n_in:n_in + nb * npg]
        o_ref = refs[n_in + nb * npg]
        scratch = refs[n_in + nb * npg + 1:]
        for i in range(nb):
            ins = []
            for kind, r in zip(kinds, refs[:n_in]):
                if kind == "tok":
                    ins.append(r.at[pl.ds(i, 1)])
                elif kind == "win":
                    ins.append(r.at[:, pl.ds(i, 1)])
                else:
                    ins.append(r)
            body(pt_ref, *ins, *pages[i * npg:(i + 1) * npg], o_ref.at[pl.ds(i, 1)],
                 *[s.at[i] for s in scratch])
    return kern


def _nsa_sample_kernel(pt_ref, q_ref, g_ref, rn_ref, wn_ref, ws_ref, *rest, past, npg, tq, page):
    pages = rest[:npg]
    o_ref = rest[npg]
    s_ref, e_ref = rest[npg + 1], rest[npg + 2]
    del pt_ref
    rows = N_HEADS * QPAD
    nc = past // CMP_BLOCK
    ns = past // SEL_BLOCK + 1
    cpp = page // CMP_BLOCK
    spp = page // SEL_BLOCK

    qp = _pad_rows(q_ref[0], QPAD)
    q4 = jnp.concatenate([qp[:, HEAD_DIM * h:HEAD_DIM * (h + 1)] for h in range(N_HEADS)], axis=0)
    iq = _iota((rows, 1), 0) & (QPAD - 1)
    qpos = past + iq

    tk = _iota((page, nc), 0)
    cb = _iota((page, nc), 1)
    kcvc = jnp.zeros((2 * HEAD_DIM, nc), F32)
    for p in range(npg):
        pm = jnp.where((tk >> 5) + cpp * p == cb, 1.0 / CMP_BLOCK, 0.0).astype(BF16)
        kcvc = kcvc + _dot_split_lhs(pages[p][0, 0, 0:2 * HEAD_DIM, :], pm, 2)
    kc = kcvc[0:HEAD_DIM].astype(BF16)
    vc = kcvc[HEAD_DIM:2 * HEAD_DIM].astype(BF16)
    cend = (_iota((1, nc), 1) + 1) * CMP_BLOCK - 1
    e, l = _softmax_parts(_dot(q4, kc), cend <= qpos)
    pc = e / l
    oc = _dot_nt(pc.astype(BF16), vc)
    pcs = pc[0:QPAD]
    for h in range(1, N_HEADS):
        pcs = pcs + pc[h * QPAD:(h + 1) * QPAD]
    pcs = jnp.concatenate([pcs] * N_HEADS, axis=0)
    pair = jnp.where((_iota((nc, LANES), 0) >> 1) == _iota((nc, LANES), 1), 1.0, 0.0).astype(BF16)
    imp = _dot_split_lhs(pcs, pair, 3)
    sel = _select_blocks(imp, qpos, ns)

    lane = _iota((1, page), 1)
    m = jnp.full((rows, 1), NEG_INF, F32)
    for p in range(npg):
        s = _dot(q4, pages[p][0, 0, 2 * HEAD_DIM:3 * HEAD_DIM, :].astype(BF16))
        selk = sel[:, spp * p:spp * p + 1]
        for j in range(1, spp):
            selk = jnp.where(lane >= j * SEL_BLOCK, sel[:, spp * p + j:spp * p + j + 1], selk)
        s = jnp.where(selk > 0.5, s, NEG_INF)
        s_ref[p * rows:(p + 1) * rows, :] = s
        m = jnp.maximum(m, jnp.max(s, axis=-1, keepdims=True))
    rn = _pad_rows(rn_ref[0], QPAD)
    jn = _iota((1, QPAD), 1)
    nmask = (jn <= iq) & (jn < tq) & (sel[:, ns - 1:ns] > 0.5)
    sn = jnp.where(nmask, _dot_nt(q4, rn[:, 128:192].astype(BF16)), NEG_INF)
    m = jnp.maximum(m, jnp.max(sn, axis=-1, keepdims=True))
    m = jnp.where(m == NEG_INF, 0.0, m)
    en = jnp.exp(sn - m)
    l = jnp.sum(en, axis=-1, keepdims=True)
    acc = _dot(en.astype(BF16), rn[:, 192:256].astype(BF16))
    e = jnp.exp(s_ref[...] - jnp.concatenate([m] * npg, axis=0))
    e_ref[...] = e.astype(BF16)
    lsum = jnp.sum(e, axis=-1, keepdims=True)
    for p in range(npg):
        l = l + lsum[p * rows:(p + 1) * rows]
        acc = acc + _dot_nt(e_ref[p * rows:(p + 1) * rows, :],
                            pages[p][0, 0, 3 * HEAD_DIM:4 * HEAD_DIM, :].astype(BF16))
    o_s = acc / jnp.maximum(l, 1e-30)

    ws = ws_ref[0, 0]
    wb = ws.shape[1]
    wn = _pad_rows(wn_ref[0], QPAD)
    kwpos = past - wb + _iota((1, wb), 1)
    wmask = (kwpos <= qpos) & (kwpos > qpos - WINDOW) & (kwpos >= 0)
    s1 = jnp.where(wmask, _dot(q4, ws[0:HEAD_DIM, :].astype(BF16)), NEG_INF)
    s2 = jnp.where((jn <= iq) & (jn < tq), _dot_nt(q4, wn[:, 0:HEAD_DIM].astype(BF16)), NEG_INF)
    m = jnp.maximum(jnp.max(s1, axis=-1, keepdims=True), jnp.max(s2, axis=-1, keepdims=True))
    m = jnp.where(m == NEG_INF, 0.0, m)
    e1 = jnp.exp(s1 - m)
    e2 = jnp.exp(s2 - m)
    l = jnp.sum(e1, axis=-1, keepdims=True) + jnp.sum(e2, axis=-1, keepdims=True)
    o_w = (_dot_nt(e1.astype(BF16), ws[HEAD_DIM:2 * HEAD_DIM, :].astype(BF16))
           + _dot(e2.astype(BF16), wn[:, HEAD_DIM:2 * HEAD_DIM].astype(BF16))) / jnp.maximum(l, 1e-30)

    g = _pad_rows(g_ref[0], QPAD)
    outs = []
    for h in range(N_HEADS):
        rs = slice(h * QPAD, (h + 1) * QPAD)
        outs.append(g[:, 3 * h:3 * h + 1] * oc[rs] + g[:, 3 * h + 1:3 * h + 2] * o_s[rs]
                    + g[:, 3 * h + 2:3 * h + 3] * o_w[rs])
    o_ref[0] = jnp.concatenate(outs, axis=1)[0:tq].astype(BF16)


def _nsa_sample(page_table, q, g, rows_new, win_new, win_state, cache, layer):
    db, tq, _ = q.shape
    npg = page_table.shape[1]
    page = cache.shape[3]
    wb = win_state.shape[3]
    rows = N_HEADS * QPAD
    nb = math.gcd(db, SEQ_PER_STEP)
    tok = lambda c: pl.BlockSpec((nb, tq, c), lambda b, pt: (b, 0, 0))
    grid_spec = pltpu.PrefetchScalarGridSpec(
        num_scalar_prefetch=1,
        grid=(db // nb,),
        in_specs=[tok(256), tok(128), tok(256), tok(128),
                  pl.BlockSpec((1, nb, 2 * HEAD_DIM, wb), lambda b, pt: (layer, b, 0, 0))]
        + _page_specs(layer, nb, npg, 4 * HEAD_DIM, page),
        out_specs=tok(256),
        scratch_shapes=[pltpu.VMEM((nb, npg * rows, page), F32), pltpu.VMEM((nb, npg * rows, page), BF16)],
    )
    body = functools.partial(_nsa_sample_kernel, past=npg * page, npg=npg, tq=tq, page=page)
    return pl.pallas_call(
        _per_sequence(body, nb, ("tok", "tok", "tok", "tok", "win"), npg),
        grid_spec=grid_spec,
        out_shape=jax.ShapeDtypeStruct((db, tq, GROUP_W), BF16),
        compiler_params=_params("arbitrary"),
        name="nsa_sample",
    )(page_table, q, g, rows_new, win_new, win_state, *([cache] * (nb * npg)))


def _sb_sample_kernel(pt_ref, q_ref, kvn_ref, *rest, npg, tq, page):
    pages = rest[:npg]
    o_ref = rest[npg]
    z_ref, a_ref = rest[npg + 1], rest[npg + 2]
    del pt_ref
    rows = N_HEADS * QPAD
    qbd = _block_diag_q(q_ref[0], N_HEADS, HEAD_DIM)
    iq = _iota((rows, 1), 0) & (QPAD - 1)

    for p in range(npg):
        z_ref[p * rows:(p + 1) * rows, :] = _dot(qbd, pages[p][0, 0, 0:GROUP_W, :].astype(BF16))
    z = z_ref[...]
    sp = _softplus(z)
    l1m = -sp
    suf = _dot_split_lhs(l1m, _later_keys(page), 2)
    tot = suf[:, 0:1] + l1m[:, 0:1]

    kvn = _pad_rows(kvn_ref[0], QPAD)
    jn = _iota((1, QPAD), 1)
    nmask = (jn < iq) & (jn < tq)
    zn = _dot_nt(qbd, kvn[:, 0:GROUP_W].astype(BF16))
    spn = _softplus(zn)
    l1n = jnp.where(nmask, -spn, 0.0)
    sufn = _dot_split_lhs(l1n, _later_keys(QPAD), 2)
    an = jnp.where(nmask, jnp.exp(zn - spn + sufn), 0.0)
    acc = _dot(an.astype(BF16), kvn[:, GROUP_W:2 * GROUP_W].astype(BF16))

    run = jnp.sum(l1n, axis=-1, keepdims=True)
    carries = [None] * npg
    for p in range(npg - 1, -1, -1):
        carries[p] = run
        run = run + tot[p * rows:(p + 1) * rows]
    a_ref[...] = jnp.exp(z - sp + suf + jnp.concatenate(carries, axis=0)).astype(BF16)
    for p in range(npg):
        acc = acc + _dot_nt(a_ref[p * rows:(p + 1) * rows, :],
                            pages[p][0, 0, GROUP_W:2 * GROUP_W, :].astype(BF16))

    keep = (_iota(acc.shape, 0) >> 3) == (_iota(acc.shape, 1) >> 6)
    acc = jnp.where(keep, acc, 0.0)
    out = acc[0:QPAD]
    for h in range(1, N_HEADS):
        out = out + acc[h * QPAD:(h + 1) * QPAD]
    o_ref[0] = out[0:tq].astype(BF16)


def _sb_sample(page_table, q, kv_new, cache, layer):
    db, tq, _ = q.shape
    npg = page_table.shape[1]
    page = cache.shape[3]
    rows = N_HEADS * QPAD
    nb = math.gcd(db, SEQ_PER_STEP)
    tok = lambda c: pl.BlockSpec((nb, tq, c), lambda b, pt: (b, 0, 0))
    grid_spec = pltpu.PrefetchScalarGridSpec(
        num_scalar_prefetch=1,
        grid=(db // nb,),
        in_specs=[tok(256), tok(512)] + _page_specs(layer, nb, npg, 2 * GROUP_W, page),
        out_specs=tok(256),
        scratch_shapes=[pltpu.VMEM((nb, npg * rows, page), F32), pltpu.VMEM((nb, npg * rows, page), BF16)],
    )
    body = functools.partial(_sb_sample_kernel, npg=npg, tq=tq, page=page)
    return pl.pallas_call(
        _per_sequence(body, nb, ("tok", "tok"), npg),
        grid_spec=grid_spec,
        out_shape=jax.ShapeDtypeStruct((db, tq, GROUP_W), BF16),
        compiler_params=_params("arbitrary"),
        name="sb_sample",
    )(page_table, q, kv_new, *([cache] * (nb * npg)))


def _diff_sample_kernel(pt_ref, q_ref, kvn_ref, lam_ref, gn_ref, *rest, npg, tq, page, lam_init):
    pages = rest[:npg]
    o_ref = rest[npg]
    s_ref, e_ref = rest[npg + 1], rest[npg + 2]
    del pt_ref
    groups = 2 * N_HEADS
    rows = groups * QPAD
    qbd = _block_diag_q(q_ref[0], groups, DIFF_DQ)
    iq = _iota((rows, 1), 0) & (QPAD - 1)

    m = jnp.full((rows, 1), NEG_INF, F32)
    for p in range(npg):
        s = _dot(qbd, pages[p][0, 0, 0:GROUP_W, :].astype(BF16))
        s_ref[p * rows:(p + 1) * rows, :] = s
        m = jnp.maximum(m, jnp.max(s, axis=-1, keepdims=True))
    kvn = _pad_rows(kvn_ref[0], QPAD)
    jn = _iota((1, QPAD), 1)
    nmask = (jn <= iq) & (jn < tq)
    sn = jnp.where(nmask, _dot_nt(qbd, kvn[:, 0:GROUP_W].astype(BF16)), NEG_INF)
    m = jnp.maximum(m, jnp.max(sn, axis=-1, keepdims=True))
    en = jnp.exp(sn - m)
    l = jnp.sum(en, axis=-1, keepdims=True)
    acc = _dot(en.astype(BF16), kvn[:, GROUP_W:2 * GROUP_W].astype(BF16))
    e = jnp.exp(s_ref[...] - jnp.concatenate([m] * npg, axis=0))
    e_ref[...] = e.astype(BF16)
    lsum = jnp.sum(e, axis=-1, keepdims=True)
    for p in range(npg):
        l = l + lsum[p * rows:(p + 1) * rows]
        acc = acc + _dot_nt(e_ref[p * rows:(p + 1) * rows, :],
                            pages[p][0, 0, GROUP_W:2 * GROUP_W, :].astype(BF16))
    o = acc / jnp.maximum(l, 1e-30)

    lam = _diff_lambda(lam_ref, lam_init)
    gn = gn_ref[...]
    outs = []
    for h in range(N_HEADS):
        cs = slice(HEAD_DIM * h, HEAD_DIM * (h + 1))
        o1 = o[(2 * h) * QPAD:(2 * h + 1) * QPAD, cs]
        o2 = o[(2 * h + 1) * QPAD:(2 * h + 2) * QPAD, cs]
        outs.append(_diff_finish(o1, o2, lam, gn, lam_init))
    o_ref[0] = jnp.concatenate(outs, axis=1)[0:tq].astype(BF16)


def _diff_sample(page_table, q, kv_new, lam, gn, cache, layer, lam_init):
    db, tq, _ = q.shape
    npg = page_table.shape[1]
    page = cache.shape[3]
    rows = 2 * N_HEADS * QPAD
    nb = math.gcd(db, SEQ_PER_STEP)
    tok = lambda c: pl.BlockSpec((nb, tq, c), lambda b, pt: (b, 0, 0))
    grid_spec = pltpu.PrefetchScalarGridSpec(
        num_scalar_prefetch=1,
        grid=(db // nb,),
        in_specs=[tok(256), tok(512),
                  pl.BlockSpec((4, DIFF_DQ), lambda b, pt: (0, 0)),
                  pl.BlockSpec((1, HEAD_DIM), lambda b, pt: (0, 0))]
        + _page_specs(layer, nb, npg, 2 * GROUP_W, page),
        out_specs=tok(256),
        scratch_shapes=[pltpu.VMEM((nb, npg * rows, page), F32), pltpu.VMEM((nb, npg * rows, page), BF16)],
    )
    body = functools.partial(_diff_sample_kernel, npg=npg, tq=tq, page=page, lam_init=lam_init)
    return pl.pallas_call(
        _per_sequence(body, nb, ("tok", "tok", "const", "const"), npg),
        grid_spec=grid_spec,
        out_shape=jax.ShapeDtypeStruct((db, tq, GROUP_W), BF16),
        compiler_params=_params("arbitrary"),
        name="diff_sample",
    )(page_table, q, kv_new, lam, gn, *([cache] * (nb * npg)))


def _mix_residual(x_ref, oa_ref, ob_ref, oc_ref, od_ref, wo_ref):
    mixed = jnp.concatenate([oa_ref[...], ob_ref[...], oc_ref[...], od_ref[...]], axis=1)
    return x_ref[...] + _dot(mixed, wo_ref[...])


def _swiglu_act(h, wg_ref, wu_ref, act_ref, width, chunk, lead=()):
    for c0 in range(0, width, chunk):
        c1 = min(c0 + chunk, width)
        idx = lead + (slice(None), slice(c0, c1))
        gate = _dot(h, wg_ref[idx])
        up = _dot(h, wu_ref[idx])
        act_ref[:, c0:c1] = (_silu(gate) * up).astype(BF16)


def _ffn_kernel(x_ref, oa_ref, ob_ref, oc_ref, od_ref, wo_ref, g_ref, wg_ref, wu_ref, wd_ref,
                o_ref, act_ref):
    x = _mix_residual(x_ref, oa_ref, ob_ref, oc_ref, od_ref, wo_ref)
    h = _rmsnorm(x, g_ref[...]).astype(BF16)
    _swiglu_act(h, wg_ref, wu_ref, act_ref, wg_ref.shape[1], 512)
    o_ref[...] = x + _dot(act_ref[...], wd_ref[...])


def _ffn(x, o4, wo, g, wg, wu, wd, tm):
    n = x.shape[0]
    f = wg.shape[1]
    row = lambda c: pl.BlockSpec((tm, c), lambda i: (i, 0))
    return pl.pallas_call(
        _ffn_kernel,
        grid=(n // tm,),
        in_specs=[row(D_MODEL)] + [row(GROUP_W)] * 4
        + [_const_spec((D_MODEL, D_MODEL)), _const_spec((1, D_MODEL)),
           _const_spec((D_MODEL, f)), _const_spec((D_MODEL, f)), _const_spec((f, D_MODEL))],
        out_specs=row(D_MODEL),
        out_shape=jax.ShapeDtypeStruct((n, D_MODEL), F32),
        scratch_shapes=[pltpu.VMEM((tm, f), BF16)],
        compiler_params=_params("parallel"),
        name="ffn",
    )(x, *o4, wo, g, wg, wu, wd)


def _moe_kernel(x_ref, oa_ref, ob_ref, oc_ref, od_ref, wo_ref, g_ref, wr_ref, wg_ref, wu_ref, wd_ref,
                o_ref, xs_ref, h_ref, gate_ref, acc_ref, act_ref):
    e = pl.program_id(1)
    lane = _iota((1, LANES), 1)

    @pl.when(e == 0)
    def _():
        x = _mix_residual(x_ref, oa_ref, ob_ref, oc_ref, od_ref, wo_ref)
        xs_ref[...] = x
        h = _rmsnorm(x, g_ref[...]).astype(BF16)
        h_ref[...] = h
        lanef = lane.astype(F32)
        logits = jnp.where(lane < N_EXPERTS, _dot(h, wr_ref[...]), NEG_INF)
        v1 = jnp.max(logits, axis=-1, keepdims=True)
        i1 = jnp.min(jnp.where(logits == v1, lanef, float(LANES)), axis=-1, keepdims=True)
        rest = jnp.where(lanef == i1, NEG_INF, logits)
        v2 = jnp.max(rest, axis=-1, keepdims=True)
        i2 = jnp.min(jnp.where(rest == v2, lanef, float(LANES)), axis=-1, keepdims=True)
        t = jnp.exp(v2 - v1)
        gate_ref[...] = (jnp.where(lanef == i1, 1.0 / (1.0 + t), 0.0)
                         + jnp.where(lanef == i2, t / (1.0 + t), 0.0))
        acc_ref[...] = jnp.zeros(acc_ref.shape, F32)

    gcol = jnp.sum(jnp.where(lane == e, gate_ref[...], 0.0), axis=-1, keepdims=True)
    _swiglu_act(h_ref[...], wg_ref, wu_ref, act_ref, wg_ref.shape[2], 512, lead=(0,))
    acc_ref[...] += gcol * _dot(act_ref[...], wd_ref[0])

    @pl.when(e == pl.num_programs(1) - 1)
    def _():
        o_ref[...] = xs_ref[...] + acc_ref[...]


def _moe(x, o4, wo, g, wr, wg, wu, wd, tm):
    n = x.shape[0]
    ne, _, f = wg.shape
    row = lambda c: pl.BlockSpec((tm, c), lambda i, e: (i, 0))
    const = lambda shape: pl.BlockSpec(shape, lambda i, e: (0,) * len(shape))
    return pl.pallas_call(
        _moe_kernel,
        grid=(n // tm, ne),
        in_specs=[row(D_MODEL)] + [row(GROUP_W)] * 4
        + [const((D_MODEL, D_MODEL)), const((1, D_MODEL)), const((D_MODEL, LANES)),
           pl.BlockSpec((1, D_MODEL, f), lambda i, e: (e, 0, 0)),
           pl.BlockSpec((1, D_MODEL, f), lambda i, e: (e, 0, 0)),
           pl.BlockSpec((1, f, D_MODEL), lambda i, e: (e, 0, 0))],
        out_specs=row(D_MODEL),
        out_shape=jax.ShapeDtypeStruct((n, D_MODEL), F32),
        scratch_shapes=[pltpu.VMEM((tm, D_MODEL), F32), pltpu.VMEM((tm, D_MODEL), BF16),
                        pltpu.VMEM((tm, LANES), F32), pltpu.VMEM((tm, D_MODEL), F32),
                        pltpu.VMEM((tm, f), BF16)],
        compiler_params=_params("parallel", "arbitrary"),
        name="moe",
    )(x, *o4, wo, g, wr, wg, wu, wd)


def _final_norm_kernel(x_ref, g_ref, o_ref):
    o_ref[...] = _rmsnorm(x_ref[...], g_ref[...])


def _final_norm(x, g, tm):
    n = x.shape[0]
    row = pl.BlockSpec((tm, D_MODEL), lambda i: (i, 0))
    return pl.pallas_call(
        _final_norm_kernel,
        grid=(n // tm,),
        in_specs=[row, _const_spec((1, D_MODEL))],
        out_specs=row,
        out_shape=jax.ShapeDtypeStruct((n, D_MODEL), F32),
        compiler_params=_params("parallel"),
        name="final_norm",
    )(x, g)


def _row_tile(n, cap):
    t = min(n, cap)
    while n % t:
        t //= 2
    return t


def _tokens_minor(a, lead):
    nd = a.ndim
    perm = tuple(range(lead)) + tuple(range(lead + 1, nd)) + (lead,)
    t = jnp.transpose(a, perm)
    return t.reshape(t.shape[:lead] + (-1, t.shape[-1]))


@jax.jit
def kernel(x_prompt, x_sample, cache_nsa, cache_sb, cache_diff, state_nsa_win, state_pool, page_table,
           norm_mix, w_in, w_out, diff_lambda, diff_norm, pool_w, pool_scale, norm_ffn,
           ffn_gate, ffn_up, ffn_down, router_w, moe_gate, moe_up, moe_down, norm_final):
    b, s, d = x_prompt.shape
    db, t, _ = x_sample.shape
    depth, n_pool, page = cache_nsa.shape[:3]
    past = page_table.shape[1] * page
    wb = state_nsa_win.shape[2]
    wkeep = min(WINDOW, s)
    np_, ns_ = b * s, db * t
    tm_p = _row_tile(s, 512)
    tm_s = _row_tile(ns_, 512)

    tabs_p = _rope_tables(jnp.arange(s))
    tabs_s = _rope_tables(past + (jnp.arange(tm_s) % t))
    cache_nsa_t = _tokens_minor(cache_nsa, 2)
    cache_sb_t = _tokens_minor(cache_sb, 2)
    cache_diff_t = _tokens_minor(cache_diff, 2)
    state_win_t = _tokens_minor(state_nsa_win, 2)

    xp = x_prompt.reshape(np_, d)
    xs = x_sample.reshape(ns_, d)
    outs = [[] for _ in range(10)]
    for l in range(depth):
        lam_init = 0.8 - 0.6 * math.exp(-0.3 * l)
        g_mix = norm_mix[l].reshape(1, d)
        w_l = _prep_w_in(w_in[l])
        lam_l = diff_lambda[l]
        gn_l = diff_norm[l].reshape(1, HEAD_DIM)
        wp_l = _pool_block_diag(pool_w[l])
        ps_l = pool_scale[l].reshape(1, GROUP_W)

        qa, rows, win, gate, qb, kvb, qc, kvc, du = _project(xp, g_mix, w_l, tabs_p, tm_p)
        qa_s, rows_s, win_s, gate_s, qb_s, kvb_s, qc_s, kvc_s, du_s = _project(xs, g_mix, w_l, tabs_s, tm_s)

        p3 = lambda a: a.reshape(b, s, a.shape[-1])
        s3 = lambda a: a.reshape(db, t, a.shape[-1])
        oa = _nsa_prompt(p3(qa), p3(gate), p3(rows), p3(win))
        ob = _sb_prompt(p3(qb), p3(kvb))
        oc = _diff_prompt(p3(qc), p3(kvc), lam_l, gn_l, lam_init)
        od = _pool(p3(du), wp_l, ps_l, 0, 0)

        oa_s = _nsa_sample(page_table, s3(qa_s), s3(gate_s), s3(rows_s), s3(win_s), state_win_t,
                           cache_nsa_t, l)
        ob_s = _sb_sample(page_table, s3(qb_s), s3(kvb_s), cache_sb_t, l)
        oc_s = _diff_sample(page_table, s3(qc_s), s3(kvc_s), lam_l, gn_l, cache_diff_t, l, lam_init)
        u_ext = jnp.concatenate([state_pool[l], s3(du_s)], axis=1)
        od_s = _pool(u_ext, wp_l, ps_l, POOL_HIST, past)

        outs[0].append(rows.reshape(b, s, 4, HEAD_DIM))
        outs[1].append(rows_s.reshape(db, t, 4, HEAD_DIM))
        outs[2].append(kvb.reshape(b, s, 2, N_HEADS, HEAD_DIM))
        outs[3].append(kvb_s.reshape(db, t, 2, N_HEADS, HEAD_DIM))
        outs[4].append(kvc.reshape(b, s, 2, N_HEADS, HEAD_DIM))
        outs[5].append(kvc_s.reshape(db, t, 2, N_HEADS, HEAD_DIM))
        outs[6].append(win.reshape(b, s, 2, HEAD_DIM)[:, s - wkeep:])
        outs[7].append(jnp.concatenate([state_nsa_win[l], win_s.reshape(db, t, 2, HEAD_DIM)], axis=1)[:, -wb:])
        outs[8].append(p3(du)[:, s - POOL_HIST:])
        outs[9].append(u_ext[:, -POOL_HIST:])

        o4_p = [a.reshape(np_, GROUP_W) for a in (oa, ob, oc, od)]
        o4_s = [a.reshape(ns_, GROUP_W) for a in (oa_s, ob_s, oc_s, od_s)]
        wo = w_out[l].astype(BF16)
        g_ffn = norm_ffn[l].reshape(1, d)
        i = l // 2
        if l % 2 == 0:
            wg, wu, wd = ffn_gate[i].astype(BF16), ffn_up[i].astype(BF16), ffn_down[i].astype(BF16)
            xp = _ffn(xp, o4_p, wo, g_ffn, wg, wu, wd, tm_p)
            xs = _ffn(xs, o4_s, wo, g_ffn, wg, wu, wd, tm_s)
        else:
            wr = jnp.pad(router_w[i], ((0, 0), (0, LANES - N_EXPERTS))).astype(BF16)
            wg, wu, wd = moe_gate[i].astype(BF16), moe_up[i].astype(BF16), moe_down[i].astype(BF16)
            xp = _moe(xp, o4_p, wo, g_ffn, wr, wg, wu, wd, tm_p)
            xs = _moe(xs, o4_s, wo, g_ffn, wr, wg, wu, wd, tm_s)

    g_fin = norm_final.reshape(1, d)
    y_p = _final_norm(xp, g_fin, tm_p).reshape(b, s, d)
    y_s = _final_norm(xs, g_fin, tm_s).reshape(db, t, d)
    return (y_p, y_s) + tuple(jnp.stack(o) for o in outs)
```

```python
import functools
import math

import numpy as np
import jax
import jax.numpy as jnp
from jax import lax
from jax.experimental import pallas as pl
from jax.experimental.pallas import tpu as pltpu

F32 = jnp.float32
BF16 = jnp.bfloat16
I32 = jnp.int32

D_MODEL = 1024
HEAD_DIM = 64
GROUP_W = D_MODEL // 4
N_HEADS = GROUP_W // HEAD_DIM
DIFF_DQ = HEAD_DIM // 2
ROPE_THETA = 500000.0
CMP_BLOCK = 32
SEL_BLOCK = 64
SEL_TOPK = 8
WINDOW = 512
FORCE_SCORE = 1.0e4
POOL_WINDOWS = (2, 4, 8, 16)
POOL_GW = GROUP_W // len(POOL_WINDOWS)
POOL_HIST = max(POOL_WINDOWS) - 1
N_EXPERTS = 8
NORM_EPS = 1e-6
PROJ_SIZES = (256, 64, 64, 64, 64, 64, 64, 12, 256, 256, 256, 256, 256, 256, 256)

QBLOCK = 128
SB_BLOCK = 256
N_EXTENTS = 8
LANES = 128
SUBLANES = 8
VMEM_LIMIT = 56 * 1024 * 1024

C_AQ, C_ROWS, C_WIN, C_GATE, C_BQ, C_BKV, C_CQ, C_CKV, C_DU, C_END = (
    0, 256, 512, 640, 768, 1024, 1536, 1792, 2304, 2560)

NEG_INF = float("-inf")
QK_SCALE = HEAD_DIM ** -0.5
DIFF_SCALE = DIFF_DQ ** -0.5
EXP_UNDERFLOW = -104.0


def _dot(a, b):
    return jnp.dot(a, b, preferred_element_type=F32)


def _dot_nt(a, b):
    return lax.dot_general(a, b, (((1,), (1,)), ((), ())), preferred_element_type=F32)


def _split_bf16(x, parts):
    out = []
    r = x
    for _ in range(parts):
        p = r.astype(BF16)
        out.append(p)
        r = r - p.astype(F32)
    return out


def _dot_split_lhs(x, w, parts):
    acc = None
    for p in _split_bf16(x, parts):
        t = _dot(p, w)
        acc = t if acc is None else acc + t
    return acc


def _dot_split_rhs(w, x, parts):
    acc = None
    for p in _split_bf16(x, parts):
        t = _dot(w, p)
        acc = t if acc is None else acc + t
    return acc


def _iota(shape, dim):
    return lax.broadcasted_iota(I32, shape, dim)


def _later_keys(n):
    return jnp.where(_iota((n, n), 0) > _iota((n, n), 1), 1.0, 0.0).astype(BF16)


def _rmsnorm(x, g):
    ms = jnp.mean(x * x, axis=-1, keepdims=True)
    return x * lax.rsqrt(ms + NORM_EPS) * g


def _softmax_parts(s, mask):
    s = jnp.where(mask, s, NEG_INF)
    m = jnp.max(s, axis=-1, keepdims=True)
    m = jnp.where(m == NEG_INF, 0.0, m)
    e = jnp.where(mask, jnp.exp(s - m), 0.0)
    l = jnp.maximum(jnp.sum(e, axis=-1, keepdims=True), 1e-30)
    return e, l


def _softplus(z):
    return jnp.maximum(z, 0.0) + jnp.log1p(jnp.exp(-jnp.abs(z)))


def _silu(x):
    return x / (1.0 + jnp.exp(-x))


def _params(*sem):
    return pltpu.CompilerParams(dimension_semantics=sem, vmem_limit_bytes=VMEM_LIMIT)


def _const_spec(shape):
    nd = len(shape)
    return pl.BlockSpec(shape, lambda *_: (0,) * nd)


def _rope_tile(f, c, s1, s2, shift):
    return f * c + pltpu.roll(f, shift, 1) * s1 + pltpu.roll(f, LANES - shift, 1) * s2


def _proj_kernel(x_ref, g_ref, w_ref, ca_ref, sa1_ref, sa2_ref, cc_ref, sc1_ref, sc2_ref,
                 qa_ref, rows_ref, win_ref, gate_ref, qb_ref, kvb_ref, qc_ref, kvc_ref, du_ref):
    h = _rmsnorm(x_ref[...], g_ref[...]).astype(BF16)

    def mm(c0, c1):
        return _dot(h, w_ref[:, c0:c1])

    ca, sa1, sa2 = ca_ref[...], sa1_ref[...], sa2_ref[...]
    cc, sc1, sc2 = cc_ref[...], sc1_ref[...], sc2_ref[...]
    first = _iota(ca.shape, 1) < HEAD_DIM
    cb = jnp.where(first, ca, 1.0)
    sb1 = jnp.where(first, sa1, 0.0)
    sb2 = jnp.where(first, sa2, 0.0)
    ha = HEAD_DIM // 8
    hc = DIFF_DQ // 8

    for t in range(2):
        f = mm(C_AQ + t * LANES, C_AQ + (t + 1) * LANES)
        qa_ref[:, t * LANES:(t + 1) * LANES] = (_rope_tile(f, ca, sa1, sa2, ha) * QK_SCALE).astype(BF16)
    for t in range(2):
        f = mm(C_ROWS + t * LANES, C_ROWS + (t + 1) * LANES)
        rows_ref[:, t * LANES:(t + 1) * LANES] = _rope_tile(f, cb, sb1, sb2, ha)
    win_ref[...] = _rope_tile(mm(C_WIN, C_GATE), cb, sb1, sb2, ha)
    gate_ref[...] = 1.0 / (1.0 + jnp.exp(-mm(C_GATE, C_BQ)))
    qb_ref[...] = (mm(C_BQ, C_BKV) * QK_SCALE).astype(BF16)
    kvb_ref[...] = mm(C_BKV, C_CQ)
    for t in range(2):
        f = mm(C_CQ + t * LANES, C_CQ + (t + 1) * LANES)
        qc_ref[:, t * LANES:(t + 1) * LANES] = (_rope_tile(f, cc, sc1, sc2, hc) * DIFF_SCALE).astype(BF16)
    for t in range(2):
        f = mm(C_CKV + t * LANES, C_CKV + (t + 1) * LANES)
        kvc_ref[:, t * LANES:(t + 1) * LANES] = _rope_tile(f, cc, sc1, sc2, hc)
    kvc_ref[:, 2 * LANES:4 * LANES] = mm(C_CKV + 2 * LANES, C_DU)
    du_ref[...] = mm(C_DU, C_END)


def _project(x, g, w, tabs, tm):
    n = x.shape[0]
    p = tabs[0].shape[0]
    nper = p // tm
    row = lambda c: pl.BlockSpec((tm, c), lambda i: (i, 0))
    tab = pl.BlockSpec((tm, LANES), lambda i: (i % nper, 0))
    outs = [(256, BF16), (256, F32), (128, F32), (128, F32), (256, BF16), (512, F32),
            (256, BF16), (512, F32), (256, F32)]
    return pl.pallas_call(
        _proj_kernel,
        grid=(n // tm,),
        in_specs=[row(D_MODEL), _const_spec((1, D_MODEL)), _const_spec((D_MODEL, C_END))] + [tab] * 6,
        out_specs=[row(c) for c, _ in outs],
        out_shape=[jax.ShapeDtypeStruct((n, c), d) for c, d in outs],
        compiler_params=_params("parallel"),
        name="proj",
    )(x, g, w, *tabs)


def _rope_tables(pos):
    pos = pos.astype(F32)[:, None]
    lane = np.arange(LANES)
    out = []
    for width in (HEAD_DIM, DIFF_DQ):
        half = width // 8
        d = lane % width
        inv = ROPE_THETA ** (-jnp.arange(half, dtype=F32) / half)
        ang = pos * inv[None, :]
        cos, sin = jnp.cos(ang), jnp.sin(ang)
        is1 = d < half
        is2 = (d >= half) & (d < 2 * half)
        fidx = np.where(is1, d, np.where(is2, d - half, 0))
        cosl, sinl = cos[:, fidx], sin[:, fidx]
        rot = jnp.asarray(is1 | is2)[None, :]
        out.append(jnp.where(rot, cosl, 1.0))
        out.append(jnp.where(jnp.asarray(is2)[None, :], sinl, 0.0))
        out.append(jnp.where(jnp.asarray(is1)[None, :], -sinl, 0.0))
    return out


def _prep_w_in(w):
    cuts = np.cumsum(PROJ_SIZES)[:-1].tolist()
    (aq, akc, avc, aks, avs, akw, avw, ag, bq, bk, bv, cq, ck, cv, du) = jnp.split(w, cuts, axis=-1)
    agp = jnp.pad(ag, ((0, 0), (0, LANES - ag.shape[1])))
    return jnp.concatenate([aq, akc, avc, aks, avs, akw, avw, agp, bq, bk, bv, cq, ck, cv, du],
                           axis=-1).astype(BF16)


def _select_blocks(imp, qpos, ns):
    w = imp.shape[1]
    blk = _iota((1, w), 1)
    cur = qpos >> 6
    forced = (blk == 0) | (blk == cur) | (blk == cur - 1)
    valid = (blk * SEL_BLOCK <= qpos) & (blk < ns)
    imp = jnp.where(forced, FORCE_SCORE, imp)
    imp = jnp.where(valid, imp, NEG_INF)
    rank = jnp.zeros(imp.shape, F32)
    for i in range(ns):
        ci = imp[:, i:i + 1]
        tie = jnp.where(blk > i, 1.0, 0.0)
        rank = rank + jnp.where(ci > imp, 1.0, jnp.where(ci == imp, tie, 0.0))
    return jnp.where((rank < min(SEL_TOPK, ns)) & (blk < ns), 1.0, 0.0)


def _select_blocks_t(imp, qpos, ns):
    blk = _iota((ns, 1), 0)
    cur = qpos >> 6
    forced = (blk == 0) | (blk == cur) | (blk == cur - 1)
    imp = jnp.where(forced, FORCE_SCORE, imp)
    imp = jnp.where(blk * SEL_BLOCK <= qpos, imp, NEG_INF)
    rank = jnp.zeros(imp.shape, F32)
    for i in range(ns):
        ci = imp[i:i + 1, :]
        tie = jnp.where(blk > i, 1.0, 0.0)
        rank = rank + jnp.where(ci > imp, 1.0, jnp.where(ci == imp, tie, 0.0))
    return jnp.where(rank < min(SEL_TOPK, ns), 1.0, 0.0)


def _eye(n):
    return jnp.where(_iota((n, n), 0) == _iota((n, n), 1), 1.0, 0.0).astype(BF16)


def _transpose_bf16(x):
    return _dot_nt(_eye(x.shape[1]), x).astype(BF16)


def _softmax_keys_major(s):
    m = jnp.max(s, axis=0, keepdims=True)
    e = jnp.exp(s - m)
    return e.astype(BF16), jnp.sum(e, axis=0, keepdims=True)


def _nsa_prompt_t_kernel(q_ref, g_ref, rows_ref, win_ref, o_ref,
                         kc_ref, vct_ref, ks_ref, vst_ref, kw_ref, vwt_ref, ost_ref, *, seq):
    qi = pl.program_id(1)
    s0 = qi * QBLOCK
    nc = seq // CMP_BLOCK
    ns = seq // SEL_BLOCK
    per = (seq // QBLOCK) // N_EXTENTS
    nq = N_HEADS * QBLOCK

    def cmp_block(shape, dim):
        r = _iota(shape, dim)
        return jnp.where(r < ns, 2 * r, 2 * (r - ns) + 1)

    @pl.when(qi == 0)
    def _():
        pm = jnp.where((_iota((nc, seq), 1) >> 5) == cmp_block((nc, seq), 0),
                       1.0 / CMP_BLOCK, 0.0).astype(BF16)
        kcvc = _dot_split_rhs(pm, rows_ref[0, :, 0:LANES], 2)
        kc_ref[...] = kcvc[:, 0:HEAD_DIM].astype(BF16)
        vct_ref[...] = _transpose_bf16(kcvc[:, HEAD_DIM:2 * HEAD_DIM].astype(BF16))
        ks_ref[...] = rows_ref[0, :, 128:192].astype(BF16)
        vst_ref[...] = _transpose_bf16(rows_ref[0, :, 192:256].astype(BF16))
        kw_ref[...] = win_ref[0, :, 0:64].astype(BF16)
        for c in range(seq // QBLOCK):
            vwt_ref[c] = _transpose_bf16(win_ref[0, c * QBLOCK:(c + 1) * QBLOCK, 64:128].astype(BF16))

    q = q_ref[0]
    q4 = jnp.concatenate([q[:, HEAD_DIM * h:HEAD_DIM * (h + 1)] for h in range(N_HEADS)], axis=0)
    qpos = s0 + _iota((1, QBLOCK), 1)
    qpos4 = jnp.concatenate([qpos] * N_HEADS, axis=1)

    cmask = (cmp_block((nc, 1), 0) + 1) * CMP_BLOCK - 1 <= qpos4
    s = jnp.where(cmask, _dot_nt(kc_ref[...], q4), NEG_INF)
    m = jnp.max(s, axis=0, keepdims=True)
    m = jnp.where(m == NEG_INF, 0.0, m)
    e = jnp.where(cmask, jnp.exp(s - m), 0.0)
    pc = e / jnp.maximum(jnp.sum(e, axis=0, keepdims=True), 1e-30)
    oc = _dot(vct_ref[...], pc.astype(BF16))
    pcs = pc[:, 0:QBLOCK]
    for h in range(1, N_HEADS):
        pcs = pcs + pc[:, h * QBLOCK:(h + 1) * QBLOCK]
    sel = _select_blocks_t(pcs[0:ns] + pcs[ns:2 * ns], qpos, ns)

    for j in range(N_EXTENTS):
        ext = (j + 1) * per * QBLOCK

        @pl.when(qi // per == j)
        def _(ext=ext):
            nse = ext // SEL_BLOCK
            expand = jnp.where((_iota((ext, nse), 0) >> 6) == _iota((ext, nse), 1), 1.0, 0.0).astype(BF16)
            selk = _dot(expand, sel[0:nse].astype(BF16))
            live = (selk > 0.5) & (_iota((ext, 1), 0) <= qpos)
            bias = jnp.where(live, 0.0, NEG_INF)
            s = _dot_nt(ks_ref[0:ext, :], q4) + jnp.concatenate([bias] * N_HEADS, axis=1)
            e, l = _softmax_keys_major(s)
            ost_ref[...] = _dot(vst_ref[:, 0:ext], e) / l

    nwc = WINDOW // QBLOCK + 1
    c0 = jnp.maximum(qi - WINDOW // QBLOCK, 0)
    start = pl.multiple_of(c0 * QBLOCK, QBLOCK)
    kwpos = start + _iota((nwc * QBLOCK, 1), 0)
    wbias = jnp.where((kwpos <= qpos) & (kwpos > qpos - WINDOW), 0.0, NEG_INF)
    s = _dot_nt(kw_ref[pl.ds(start, nwc * QBLOCK), :], q4) + jnp.concatenate([wbias] * N_HEADS, axis=1)
    e, l = _softmax_keys_major(s)
    ow = _dot(vwt_ref[c0], e[0:QBLOCK])
    for c in range(1, nwc):
        ow = ow + _dot(vwt_ref[c0 + c], e[c * QBLOCK:(c + 1) * QBLOCK])
    ow = ow / l

    gt = jnp.transpose(g_ref[0])
    outs = []
    for h in range(N_HEADS):
        cs = slice(h * QBLOCK, (h + 1) * QBLOCK)
        outs.append(gt[3 * h:3 * h + 1] * oc[:, cs] + gt[3 * h + 1:3 * h + 2] * ost_ref[:, cs]
                    + gt[3 * h + 2:3 * h + 3] * ow[:, cs])
    out_t = jnp.concatenate(outs, axis=0)
    o_ref[0] = jnp.concatenate([jnp.transpose(out_t[0:LANES]), jnp.transpose(out_t[LANES:2 * LANES])],
                               axis=1).astype(BF16)


def _nsa_prompt_t(q, g, rows, win):
    b, s, _ = q.shape
    qspec = lambda c: pl.BlockSpec((1, QBLOCK, c), lambda i, j: (i, j, 0))
    full = lambda c: pl.BlockSpec((1, s, c), lambda i, j: (i, 0, 0))
    nc = s // CMP_BLOCK
    return pl.pallas_call(
        functools.partial(_nsa_prompt_t_kernel, seq=s),
        grid=(b, s // QBLOCK),
        in_specs=[qspec(256), qspec(128), full(256), full(128)],
        out_specs=qspec(256),
        out_shape=jax.ShapeDtypeStruct((b, s, GROUP_W), BF16),
        scratch_shapes=[pltpu.VMEM((nc, HEAD_DIM), BF16), pltpu.VMEM((HEAD_DIM, nc), BF16),
                        pltpu.VMEM((s, HEAD_DIM), BF16), pltpu.VMEM((HEAD_DIM, s), BF16),
                        pltpu.VMEM((s, HEAD_DIM), BF16), pltpu.VMEM((s // QBLOCK, HEAD_DIM, QBLOCK), BF16),
                        pltpu.VMEM((HEAD_DIM, N_HEADS * QBLOCK), F32)],
        compiler_params=_params("parallel", "arbitrary"),
        name="nsa_prompt",
    )(q, g, rows, win)


def _sb_prompt_kernel(q_ref, kv_ref, o_ref, kb_ref, vb_ref, r_ref, acc_ref):
    qi = pl.program_id(1)
    s0 = qi * SB_BLOCK

    @pl.when(qi == 0)
    def _():
        kb_ref[...] = kv_ref[0, :, 0:GROUP_W].astype(BF16)
        vb_ref[...] = kv_ref[0, :, GROUP_W:2 * GROUP_W].astype(BF16)

    q = q_ref[0]
    r_ref[...] = jnp.zeros(r_ref.shape, F32)
    acc_ref[...] = jnp.zeros(acc_ref.shape, F32)
    later = _later_keys(SB_BLOCK)
    qpos = s0 + _iota((SB_BLOCK, 1), 0)

    def cond(c):
        it, rmax = c
        return (it <= qi) & (rmax > EXP_UNDERFLOW)

    def body(c):
        it, _ = c
        k0 = pl.multiple_of((qi - it) * SB_BLOCK, SB_BLOCK)
        mask = (k0 + _iota((1, SB_BLOCK), 1)) < qpos
        rm = None
        for h in range(N_HEADS):
            cs = slice(HEAD_DIM * h, HEAD_DIM * (h + 1))
            z = _dot_nt(q[:, cs], kb_ref[pl.ds(k0, SB_BLOCK), cs])
            sp = _softplus(z)
            l1m = jnp.where(mask, -sp, 0.0)
            suf = _dot_split_lhs(l1m, later, 2)
            rh = r_ref[h]
            a = jnp.where(mask, jnp.exp(z - sp + suf + jnp.concatenate([rh, rh], axis=1)), 0.0)
            acc_ref[:, cs] += _dot(a.astype(BF16), vb_ref[pl.ds(k0, SB_BLOCK), cs])
            rn = rh + (suf[:, 0:1] + l1m[:, 0:1])
            r_ref[h] = rn
            rm = rn if rm is None else jnp.maximum(rm, rn)
        return it + 1, jnp.max(rm)

    lax.while_loop(cond, body, (jnp.int32(0), jnp.float32(0.0)))
    o_ref[0] = acc_ref[...].astype(BF16)


def _sb_prompt(q, kv):
    b, s, _ = q.shape
    qspec = pl.BlockSpec((1, SB_BLOCK, GROUP_W), lambda i, j: (i, j, 0))
    return pl.pallas_call(
        _sb_prompt_kernel,
        grid=(b, s // SB_BLOCK),
        in_specs=[qspec, pl.BlockSpec((1, s, 2 * GROUP_W), lambda i, j: (i, 0, 0))],
        out_specs=qspec,
        out_shape=jax.ShapeDtypeStruct((b, s, GROUP_W), BF16),
        scratch_shapes=[pltpu.VMEM((s, GROUP_W), BF16), pltpu.VMEM((s, GROUP_W), BF16),
                        pltpu.VMEM((N_HEADS, SB_BLOCK, LANES), F32), pltpu.VMEM((SB_BLOCK, GROUP_W), F32)],
        compiler_params=_params("parallel", "arbitrary"),
        name="sb_prompt",
    )(q, kv)


def _diff_lambda(lam_ref, lam_init):
    lv = lam_ref[...]
    a = jnp.sum(lv[0:1] * lv[1:2], axis=1, keepdims=True)
    b = jnp.sum(lv[2:3] * lv[3:4], axis=1, keepdims=True)
    return jnp.exp(a) - jnp.exp(b) + lam_init


def _diff_finish(o1, o2, lam, gn, lam_init):
    o = o1 - lam * o2
    return _rmsnorm(o, gn) * (1.0 - lam_init)


def _diff_prompt_kernel(q_ref, kv_ref, lam_ref, gn_ref, o_ref, kb_ref, vb_ref, *, seq, lam_init):
    qi = pl.program_id(1)
    s0 = qi * QBLOCK
    per = (seq // QBLOCK) // N_EXTENTS

    @pl.when(qi == 0)
    def _():
        kb_ref[...] = kv_ref[0, :, 0:GROUP_W].astype(BF16)
        vb_ref[...] = kv_ref[0, :, GROUP_W:2 * GROUP_W].astype(BF16)

    lam = _diff_lambda(lam_ref, lam_init)
    gn = gn_ref[...]
    q = q_ref[0]
    qpos = s0 + _iota((QBLOCK, 1), 0)

    for j in range(N_EXTENTS):
        ext = (j + 1) * per * QBLOCK

        @pl.when(qi // per == j)
        def _(ext=ext):
            bias = jnp.where(_iota((1, ext), 1) <= qpos, 0.0, NEG_INF)
            outs = []
            for h in range(N_HEADS):
                es, ls = [], []
                for m in range(2):
                    cs = slice(HEAD_DIM * h + DIFF_DQ * m, HEAD_DIM * h + DIFF_DQ * (m + 1))
                    s = _dot_nt(q[:, cs], kb_ref[0:ext, cs]) + bias
                    e = jnp.exp(s - jnp.max(s, axis=-1, keepdims=True))
                    es.append(e.astype(BF16))
                    ls.append(jnp.sum(e, axis=-1, keepdims=True))
                o = _dot(jnp.concatenate(es, axis=0), vb_ref[0:ext, HEAD_DIM * h:HEAD_DIM * (h + 1)])
                outs.append(_diff_finish(o[0:QBLOCK] / ls[0], o[QBLOCK:2 * QBLOCK] / ls[1],
                                         lam, gn, lam_init))
            o_ref[0] = jnp.concatenate(outs, axis=1).astype(BF16)


def _diff_prompt(q, kv, lam, gn, lam_init):
    b, s, _ = q.shape
    qspec = pl.BlockSpec((1, QBLOCK, GROUP_W), lambda i, j: (i, j, 0))
    return pl.pallas_call(
        functools.partial(_diff_prompt_kernel, seq=s, lam_init=lam_init),
        grid=(b, s // QBLOCK),
        in_specs=[qspec, pl.BlockSpec((1, s, 2 * GROUP_W), lambda i, j: (i, 0, 0)),
                  pl.BlockSpec((4, DIFF_DQ), lambda i, j: (0, 0)),
                  pl.BlockSpec((1, HEAD_DIM), lambda i, j: (0, 0))],
        out_specs=qspec,
        out_shape=jax.ShapeDtypeStruct((b, s, GROUP_W), BF16),
        scratch_shapes=[pltpu.VMEM((s, GROUP_W), BF16), pltpu.VMEM((s, GROUP_W), BF16)],
        compiler_params=_params("parallel", "arbitrary"),
        name="diff_prompt",
    )(q, kv, lam, gn)


def _pool_kernel(u_ref, w_ref, sc_ref, o_ref, buf_ref, *, hist, pos0):
    n_ext = u_ref.shape[1]
    pad = max(POOL_WINDOWS)
    u = u_ref[0]
    buf_ref[0:pad, :] = jnp.zeros((pad, GROUP_W), F32)
    buf_ref[pad:pad + n_ext, :] = u
    acc = u
    sums = []
    shift = 1
    for _ in POOL_WINDOWS:
        acc = acc + buf_ref[pad - shift:pad - shift + n_ext, :]
        sums.append(acc)
        shift *= 2
        if shift < pad:
            buf_ref[pad:pad + n_ext, :] = acc
    lane = _iota((1, GROUP_W), 1)
    grp = lane >> 6
    wsum = jnp.where(grp == 0, sums[0], jnp.where(grp == 1, sums[1], jnp.where(grp == 2, sums[2], sums[3])))
    wlen = jnp.where(grp == 0, 2, jnp.where(grp == 1, 4, jnp.where(grp == 2, 8, 16)))
    pos = pos0 - hist + _iota((n_ext, 1), 0)
    cnt = jnp.maximum(jnp.minimum(wlen, pos + 1), 1).astype(F32)
    diff = (wsum / cnt - u)[hist:, :]
    o_ref[0] = (_dot(diff.astype(BF16), w_ref[...]) * sc_ref[...]).astype(BF16)


def _pool(u_ext, w_bd, scale, hist, pos0):
    b, n_ext, _ = u_ext.shape
    n_out = n_ext - hist
    return pl.pallas_call(
        functools.partial(_pool_kernel, hist=hist, pos0=pos0),
        grid=(b,),
        in_specs=[pl.BlockSpec((1, n_ext, GROUP_W), lambda i: (i, 0, 0)),
                  _const_spec((GROUP_W, GROUP_W)), _const_spec((1, GROUP_W))],
        out_specs=pl.BlockSpec((1, n_out, GROUP_W), lambda i: (i, 0, 0)),
        out_shape=jax.ShapeDtypeStruct((b, n_out, GROUP_W), BF16),
        scratch_shapes=[pltpu.VMEM((max(POOL_WINDOWS) + n_ext, GROUP_W), F32)],
        compiler_params=_params("parallel"),
        name="pool",
    )(u_ext, w_bd, scale)


def _pool_block_diag(w):
    z = jnp.zeros((GROUP_W, GROUP_W), F32)
    for gi in range(len(POOL_WINDOWS)):
        z = z.at[gi * POOL_GW:(gi + 1) * POOL_GW, gi * POOL_GW:(gi + 1) * POOL_GW].set(w[gi])
    return z.astype(BF16)


QPAD = SUBLANES


def _pad_rows(x, n):
    return jnp.concatenate([x, jnp.zeros((n - x.shape[0], x.shape[1]), x.dtype)], axis=0)


def _block_diag_q(q, groups, width):
    qp = _pad_rows(q, QPAD)
    qq = jnp.concatenate([qp] * groups, axis=0)
    shape = qq.shape
    shift = int(math.log2(width))
    keep = (_iota(shape, 0) >> 3) == (_iota(shape, 1) >> shift)
    return jnp.where(keep, qq, jnp.zeros_like(qq))


SEQ_PER_STEP = 4


def _page_specs(layer, nb, npg, feat, page):
    return [pl.BlockSpec((1, 1, feat, page), lambda b, pt, i=i, j=j: (layer, pt[b * nb + i, j], 0, 0))
            for i in range(nb) for j in range(npg)]


def _per_sequence(body, nb, kinds, npg):
    def kern(pt_ref, *refs):
        n_in = len(kinds)
        pages = refs[n_in:n_in + nb * npg]
        o_ref = refs[n_in + nb * npg]
        scratch = refs[n_in + nb * npg + 1:]
        live = []
        for i in range(nb):
            ins = []
            for kind, r in zip(kinds, refs[:n_in]):
                if kind == "tok":
                    ins.append(r.at[pl.ds(i, 1)])
                elif kind == "win":
                    ins.append(r.at[:, pl.ds(i, 1)])
                else:
                    ins.append(r)
            live.append(body(pt_ref, *ins, *pages[i * npg:(i + 1) * npg], o_ref.at[pl.ds(i, 1)],
                             *[s.at[i] for s in scratch]))
        done = object()
        while live:
            live = [g for g in live if next(g, done) is not done]
    return kern


def _nsa_sample_kernel(pt_ref, q_ref, g_ref, rn_ref, wn_ref, ws_ref, *rest, past, npg, tq, page):
    pages = rest[:npg]
    o_ref = rest[npg]
    s_ref, e_ref = rest[npg + 1], rest[npg + 2]
    del pt_ref
    rows = N_HEADS * QPAD
    nc = past // CMP_BLOCK
    ns = past // SEL_BLOCK + 1
    cpp = page // CMP_BLOCK
    spp = page // SEL_BLOCK

    qp = _pad_rows(q_ref[0], QPAD)
    q4 = jnp.concatenate([qp[:, HEAD_DIM * h:HEAD_DIM * (h + 1)] for h in range(N_HEADS)], axis=0)
    iq = _iota((rows, 1), 0) & (QPAD - 1)
    qpos = past + iq

    tk = _iota((page, nc), 0)
    cb = _iota((page, nc), 1)
    kcvc = jnp.zeros((2 * HEAD_DIM, nc), F32)
    for p in range(npg):
        pm = jnp.where((tk >> 5) + cpp * p == cb, 1.0 / CMP_BLOCK, 0.0).astype(BF16)
        kcvc = kcvc + _dot_split_lhs(pages[p][0, 0, 0:2 * HEAD_DIM, :], pm, 2)
    yield
    kc = kcvc[0:HEAD_DIM].astype(BF16)
    vc = kcvc[HEAD_DIM:2 * HEAD_DIM].astype(BF16)
    cend = (_iota((1, nc), 1) + 1) * CMP_BLOCK - 1
    e, l = _softmax_parts(_dot(q4, kc), cend <= qpos)
    pc = e / l
    oc = _dot_nt(pc.astype(BF16), vc)
    pcs = pc[0:QPAD]
    for h in range(1, N_HEADS):
        pcs = pcs + pc[h * QPAD:(h + 1) * QPAD]
    pcs = jnp.concatenate([pcs] * N_HEADS, axis=0)
    pair = jnp.where((_iota((nc, LANES), 0) >> 1) == _iota((nc, LANES), 1), 1.0, 0.0).astype(BF16)
    imp = _dot_split_lhs(pcs, pair, 3)
    sel = _select_blocks(imp, qpos, ns)

    yield
    lane = _iota((1, page), 1)
    m = jnp.full((rows, 1), NEG_INF, F32)
    for p in range(npg):
        s = _dot(q4, pages[p][0, 0, 2 * HEAD_DIM:3 * HEAD_DIM, :].astype(BF16))
        selk = sel[:, spp * p:spp * p + 1]
        for j in range(1, spp):
            selk = jnp.where(lane >= j * SEL_BLOCK, sel[:, spp * p + j:spp * p + j + 1], selk)
        s = jnp.where(selk > 0.5, s, NEG_INF)
        s_ref[p * rows:(p + 1) * rows, :] = s
        m = jnp.maximum(m, jnp.max(s, axis=-1, keepdims=True))
    yield
    rn = _pad_rows(rn_ref[0], QPAD)
    jn = _iota((1, QPAD), 1)
    nmask = (jn <= iq) & (jn < tq) & (sel[:, ns - 1:ns] > 0.5)
    sn = jnp.where(nmask, _dot_nt(q4, rn[:, 128:192].astype(BF16)), NEG_INF)
    m = jnp.maximum(m, jnp.max(sn, axis=-1, keepdims=True))
    m = jnp.where(m == NEG_INF, 0.0, m)
    en = jnp.exp(sn - m)
    l = jnp.sum(en, axis=-1, keepdims=True)
    acc = _dot(en.astype(BF16), rn[:, 192:256].astype(BF16))
    e = jnp.exp(s_ref[...] - jnp.concatenate([m] * npg, axis=0))
    e_ref[...] = e.astype(BF16)
    lsum = jnp.sum(e, axis=-1, keepdims=True)
    for p in range(npg):
        l = l + lsum[p * rows:(p + 1) * rows]
        acc = acc + _dot_nt(e_ref[p * rows:(p + 1) * rows, :],
                            pages[p][0, 0, 3 * HEAD_DIM:4 * HEAD_DIM, :].astype(BF16))
    o_s = acc / jnp.maximum(l, 1e-30)

    yield
    ws = ws_ref[0, 0]
    wb = ws.shape[1]
    wn = _pad_rows(wn_ref[0], QPAD)
    kwpos = past - wb + _iota((1, wb), 1)
    wmask = (kwpos <= qpos) & (kwpos > qpos - WINDOW) & (kwpos >= 0)
    s1 = jnp.where(wmask, _dot(q4, ws[0:HEAD_DIM, :].astype(BF16)), NEG_INF)
    s2 = jnp.where((jn <= iq) & (jn < tq), _dot_nt(q4, wn[:, 0:HEAD_DIM].astype(BF16)), NEG_INF)
    m = jnp.maximum(jnp.max(s1, axis=-1, keepdims=True), jnp.max(s2, axis=-1, keepdims=True))
    m = jnp.where(m == NEG_INF, 0.0, m)
    e1 = jnp.exp(s1 - m)
    e2 = jnp.exp(s2 - m)
    l = jnp.sum(e1, axis=-1, keepdims=True) + jnp.sum(e2, axis=-1, keepdims=True)
    o_w = (_dot_nt(e1.astype(BF16), ws[HEAD_DIM:2 * HEAD_DIM, :].astype(BF16))
           + _dot(e2.astype(BF16), wn[:, HEAD_DIM:2 * HEAD_DIM].astype(BF16))) / jnp.maximum(l, 1e-30)

    g = _pad_rows(g_ref[0], QPAD)
    outs = []
    for h in range(N_HEADS):
        rs = slice(h * QPAD, (h + 1) * QPAD)
        outs.append(g[:, 3 * h:3 * h + 1] * oc[rs] + g[:, 3 * h + 1:3 * h + 2] * o_s[rs]
                    + g[:, 3 * h + 2:3 * h + 3] * o_w[rs])
    o_ref[0] = jnp.concatenate(outs, axis=1)[0:tq].astype(BF16)


def _nsa_sample(page_table, q, g, rows_new, win_new, win_state, cache, layer):
    db, tq, _ = q.shape
    npg = page_table.shape[1]
    page = cache.shape[3]
    wb = win_state.shape[3]
    rows = N_HEADS * QPAD
    nb = math.gcd(db, SEQ_PER_STEP)
    tok = lambda c: pl.BlockSpec((nb, tq, c), lambda b, pt: (b, 0, 0))
    grid_spec = pltpu.PrefetchScalarGridSpec(
        num_scalar_prefetch=1,
        grid=(db // nb,),
        in_specs=[tok(256), tok(128), tok(256), tok(128),
                  pl.BlockSpec((1, nb, 2 * HEAD_DIM, wb), lambda b, pt: (layer, b, 0, 0))]
        + _page_specs(layer, nb, npg, 4 * HEAD_DIM, page),
        out_specs=tok(256),
        scratch_shapes=[pltpu.VMEM((nb, npg * rows, page), F32), pltpu.VMEM((nb, npg * rows, page), BF16)],
    )
    body = functools.partial(_nsa_sample_kernel, past=npg * page, npg=npg, tq=tq, page=page)
    return pl.pallas_call(
        _per_sequence(body, nb, ("tok", "tok", "tok", "tok", "win"), npg),
        grid_spec=grid_spec,
        out_shape=jax.ShapeDtypeStruct((db, tq, GROUP_W), BF16),
        compiler_params=_params("arbitrary"),
        name="nsa_sample",
    )(page_table, q, g, rows_new, win_new, win_state, *([cache] * (nb * npg)))


def _sb_sample_kernel(pt_ref, q_ref, kvn_ref, *rest, npg, tq, page):
    pages = rest[:npg]
    o_ref = rest[npg]
    z_ref, a_ref = rest[npg + 1], rest[npg + 2]
    del pt_ref
    rows = N_HEADS * QPAD
    qbd = _block_diag_q(q_ref[0], N_HEADS, HEAD_DIM)
    iq = _iota((rows, 1), 0) & (QPAD - 1)

    for p in range(npg):
        z_ref[p * rows:(p + 1) * rows, :] = _dot(qbd, pages[p][0, 0, 0:GROUP_W, :].astype(BF16))
    yield
    z = z_ref[...]
    sp = _softplus(z)
    l1m = -sp
    suf = _dot_split_lhs(l1m, _later_keys(page), 2)
    tot = suf[:, 0:1] + l1m[:, 0:1]

    yield
    kvn = _pad_rows(kvn_ref[0], QPAD)
    jn = _iota((1, QPAD), 1)
    nmask = (jn < iq) & (jn < tq)
    zn = _dot_nt(qbd, kvn[:, 0:GROUP_W].astype(BF16))
    spn = _softplus(zn)
    l1n = jnp.where(nmask, -spn, 0.0)
    sufn = _dot_split_lhs(l1n, _later_keys(QPAD), 2)
    an = jnp.where(nmask, jnp.exp(zn - spn + sufn), 0.0)
    acc = _dot(an.astype(BF16), kvn[:, GROUP_W:2 * GROUP_W].astype(BF16))

    run = jnp.sum(l1n, axis=-1, keepdims=True)
    carries = [None] * npg
    for p in range(npg - 1, -1, -1):
        carries[p] = run
        run = run + tot[p * rows:(p + 1) * rows]
    a_ref[...] = jnp.exp(z - sp + suf + jnp.concatenate(carries, axis=0)).astype(BF16)
    yield
    for p in range(npg):
        acc = acc + _dot_nt(a_ref[p * rows:(p + 1) * rows, :],
                            pages[p][0, 0, GROUP_W:2 * GROUP_W, :].astype(BF16))

    keep = (_iota(acc.shape, 0) >> 3) == (_iota(acc.shape, 1) >> 6)
    acc = jnp.where(keep, acc, 0.0)
    out = acc[0:QPAD]
    for h in range(1, N_HEADS):
        out = out + acc[h * QPAD:(h + 1) * QPAD]
    o_ref[0] = out[0:tq].astype(BF16)


def _sb_sample(page_table, q, kv_new, cache, layer):
    db, tq, _ = q.shape
    npg = page_table.shape[1]
    page = cache.shape[3]
    rows = N_HEADS * QPAD
    nb = math.gcd(db, SEQ_PER_STEP)
    tok = lambda c: pl.BlockSpec((nb, tq, c), lambda b, pt: (b, 0, 0))
    grid_spec = pltpu.PrefetchScalarGridSpec(
        num_scalar_prefetch=1,
        grid=(db // nb,),
        in_specs=[tok(256), tok(512)] + _page_specs(layer, nb, npg, 2 * GROUP_W, page),
        out_specs=tok(256),
        scratch_shapes=[pltpu.VMEM((nb, npg * rows, page), F32), pltpu.VMEM((nb, npg * rows, page), BF16)],
    )
    body = functools.partial(_sb_sample_kernel, npg=npg, tq=tq, page=page)
    return pl.pallas_call(
        _per_sequence(body, nb, ("tok", "tok"), npg),
        grid_spec=grid_spec,
        out_shape=jax.ShapeDtypeStruct((db, tq, GROUP_W), BF16),
        compiler_params=_params("arbitrary"),
        name="sb_sample",
    )(page_table, q, kv_new, *([cache] * (nb * npg)))


def _diff_sample_kernel(pt_ref, q_ref, kvn_ref, lam_ref, gn_ref, *rest, npg, tq, page, lam_init):
    pages = rest[:npg]
    o_ref = rest[npg]
    s_ref, e_ref = rest[npg + 1], rest[npg + 2]
    del pt_ref
    groups = 2 * N_HEADS
    rows = groups * QPAD
    qbd = _block_diag_q(q_ref[0], groups, DIFF_DQ)
    iq = _iota((rows, 1), 0) & (QPAD - 1)

    m = jnp.full((rows, 1), NEG_INF, F32)
    for p in range(npg):
        s = _dot(qbd, pages[p][0, 0, 0:GROUP_W, :].astype(BF16))
        s_ref[p * rows:(p + 1) * rows, :] = s
        m = jnp.maximum(m, jnp.max(s, axis=-1, keepdims=True))
    yield
    kvn = _pad_rows(kvn_ref[0], QPAD)
    jn = _iota((1, QPAD), 1)
    nmask = (jn <= iq) & (jn < tq)
    sn = jnp.where(nmask, _dot_nt(qbd, kvn[:, 0:GROUP_W].astype(BF16)), NEG_INF)
    m = jnp.maximum(m, jnp.max(sn, axis=-1, keepdims=True))
    en = jnp.exp(sn - m)
    l = jnp.sum(en, axis=-1, keepdims=True)
    acc = _dot(en.astype(BF16), kvn[:, GROUP_W:2 * GROUP_W].astype(BF16))
    e = jnp.exp(s_ref[...] - jnp.concatenate([m] * npg, axis=0))
    e_ref[...] = e.astype(BF16)
    lsum = jnp.sum(e, axis=-1, keepdims=True)
    for p in range(npg):
        l = l + lsum[p * rows:(p + 1) * rows]
        acc = acc + _dot_nt(e_ref[p * rows:(p + 1) * rows, :],
                            pages[p][0, 0, GROUP_W:2 * GROUP_W, :].astype(BF16))
    o = acc / jnp.maximum(l, 1e-30)

    yield
    lam = _diff_lambda(lam_ref, lam_init)
    gn = gn_ref[...]
    outs = []
    for h in range(N_HEADS):
        cs = slice(HEAD_DIM * h, HEAD_DIM * (h + 1))
        o1 = o[(2 * h) * QPAD:(2 * h + 1) * QPAD, cs]
        o2 = o[(2 * h + 1) * QPAD:(2 * h + 2) * QPAD, cs]
        outs.append(_diff_finish(o1, o2, lam, gn, lam_init))
    o_ref[0] = jnp.concatenate(outs, axis=1)[0:tq].astype(BF16)


def _diff_sample(page_table, q, kv_new, lam, gn, cache, layer, lam_init):
    db, tq, _ = q.shape
    npg = page_table.shape[1]
    page = cache.shape[3]
    rows = 2 * N_HEADS * QPAD
    nb = math.gcd(db, SEQ_PER_STEP)
    tok = lambda c: pl.BlockSpec((nb, tq, c), lambda b, pt: (b, 0, 0))
    grid_spec = pltpu.PrefetchScalarGridSpec(
        num_scalar_prefetch=1,
        grid=(db // nb,),
        in_specs=[tok(256), tok(512),
                  pl.BlockSpec((4, DIFF_DQ), lambda b, pt: (0, 0)),
                  pl.BlockSpec((1, HEAD_DIM), lambda b, pt: (0, 0))]
        + _page_specs(layer, nb, npg, 2 * GROUP_W, page),
        out_specs=tok(256),
        scratch_shapes=[pltpu.VMEM((nb, npg * rows, page), F32), pltpu.VMEM((nb, npg * rows, page), BF16)],
    )
    body = functools.partial(_diff_sample_kernel, npg=npg, tq=tq, page=page, lam_init=lam_init)
    return pl.pallas_call(
        _per_sequence(body, nb, ("tok", "tok", "const", "const"), npg),
        grid_spec=grid_spec,
        out_shape=jax.ShapeDtypeStruct((db, tq, GROUP_W), BF16),
        compiler_params=_params("arbitrary"),
        name="diff_sample",
    )(page_table, q, kv_new, lam, gn, *([cache] * (nb * npg)))


def _mix_residual(x_ref, oa_ref, ob_ref, oc_ref, od_ref, wo_ref):
    mixed = jnp.concatenate([oa_ref[...], ob_ref[...], oc_ref[...], od_ref[...]], axis=1)
    return x_ref[...] + _dot(mixed, wo_ref[...])


def _swiglu_act(h, wg_ref, wu_ref, act_ref, width, chunk, lead=()):
    for c0 in range(0, width, chunk):
        c1 = min(c0 + chunk, width)
        idx = lead + (slice(None), slice(c0, c1))
        gate = _dot(h, wg_ref[idx])
        up = _dot(h, wu_ref[idx])
        act_ref[:, c0:c1] = (_silu(gate) * up).astype(BF16)


def _ffn_kernel(x_ref, oa_ref, ob_ref, oc_ref, od_ref, wo_ref, g_ref, wg_ref, wu_ref, wd_ref,
                o_ref, act_ref):
    x = _mix_residual(x_ref, oa_ref, ob_ref, oc_ref, od_ref, wo_ref)
    h = _rmsnorm(x, g_ref[...]).astype(BF16)
    _swiglu_act(h, wg_ref, wu_ref, act_ref, wg_ref.shape[1], 512)
    o_ref[...] = x + _dot(act_ref[...], wd_ref[...])


def _ffn(x, o4, wo, g, wg, wu, wd, tm):
    n = x.shape[0]
    f = wg.shape[1]
    row = lambda c: pl.BlockSpec((tm, c), lambda i: (i, 0))
    return pl.pallas_call(
        _ffn_kernel,
        grid=(n // tm,),
        in_specs=[row(D_MODEL)] + [row(GROUP_W)] * 4
        + [_const_spec((D_MODEL, D_MODEL)), _const_spec((1, D_MODEL)),
           _const_spec((D_MODEL, f)), _const_spec((D_MODEL, f)), _const_spec((f, D_MODEL))],
        out_specs=row(D_MODEL),
        out_shape=jax.ShapeDtypeStruct((n, D_MODEL), F32),
        scratch_shapes=[pltpu.VMEM((tm, f), BF16)],
        compiler_params=_params("parallel"),
        name="ffn",
    )(x, *o4, wo, g, wg, wu, wd)


def _moe_kernel(x_ref, oa_ref, ob_ref, oc_ref, od_ref, wo_ref, g_ref, wr_ref, wg_ref, wu_ref, wd_ref,
                o_ref, xs_ref, h_ref, gate_ref, acc_ref, act_ref):
    e = pl.program_id(1)
    lane = _iota((1, LANES), 1)

    @pl.when(e == 0)
    def _():
        x = _mix_residual(x_ref, oa_ref, ob_ref, oc_ref, od_ref, wo_ref)
        xs_ref[...] = x
        h = _rmsnorm(x, g_ref[...]).astype(BF16)
        h_ref[...] = h
        lanef = lane.astype(F32)
        logits = jnp.where(lane < N_EXPERTS, _dot(h, wr_ref[...]), NEG_INF)
        v1 = jnp.max(logits, axis=-1, keepdims=True)
        i1 = jnp.min(jnp.where(logits == v1, lanef, float(LANES)), axis=-1, keepdims=True)
        rest = jnp.where(lanef == i1, NEG_INF, logits)
        v2 = jnp.max(rest, axis=-1, keepdims=True)
        i2 = jnp.min(jnp.where(rest == v2, lanef, float(LANES)), axis=-1, keepdims=True)
        t = jnp.exp(v2 - v1)
        gate_ref[...] = (jnp.where(lanef == i1, 1.0 / (1.0 + t), 0.0)
                         + jnp.where(lanef == i2, t / (1.0 + t), 0.0))
        acc_ref[...] = jnp.zeros(acc_ref.shape, F32)

    gcol = jnp.sum(jnp.where(lane == e, gate_ref[...], 0.0), axis=-1, keepdims=True)
    _swiglu_act(h_ref[...], wg_ref, wu_ref, act_ref, wg_ref.shape[2], 512, lead=(0,))
    acc_ref[...] += gcol * _dot(act_ref[...], wd_ref[0])

    @pl.when(e == pl.num_programs(1) - 1)
    def _():
        o_ref[...] = xs_ref[...] + acc_ref[...]


def _moe(x, o4, wo, g, wr, wg, wu, wd, tm):
    n = x.shape[0]
    ne, _, f = wg.shape
    row = lambda c: pl.BlockSpec((tm, c), lambda i, e: (i, 0))
    const = lambda shape: pl.BlockSpec(shape, lambda i, e: (0,) * len(shape))
    return pl.pallas_call(
        _moe_kernel,
        grid=(n // tm, ne),
        in_specs=[row(D_MODEL)] + [row(GROUP_W)] * 4
        + [const((D_MODEL, D_MODEL)), const((1, D_MODEL)), const((D_MODEL, LANES)),
           pl.BlockSpec((1, D_MODEL, f), lambda i, e: (e, 0, 0)),
           pl.BlockSpec((1, D_MODEL, f), lambda i, e: (e, 0, 0)),
           pl.BlockSpec((1, f, D_MODEL), lambda i, e: (e, 0, 0))],
        out_specs=row(D_MODEL),
        out_shape=jax.ShapeDtypeStruct((n, D_MODEL), F32),
        scratch_shapes=[pltpu.VMEM((tm, D_MODEL), F32), pltpu.VMEM((tm, D_MODEL), BF16),
                        pltpu.VMEM((tm, LANES), F32), pltpu.VMEM((tm, D_MODEL), F32),
                        pltpu.VMEM((tm, f), BF16)],
        compiler_params=_params("parallel", "arbitrary"),
        name="moe",
    )(x, *o4, wo, g, wr, wg, wu, wd)


def _final_norm_kernel(x_ref, g_ref, o_ref):
    o_ref[...] = _rmsnorm(x_ref[...], g_ref[...])


def _final_norm(x, g, tm):
    n = x.shape[0]
    row = pl.BlockSpec((tm, D_MODEL), lambda i: (i, 0))
    return pl.pallas_call(
        _final_norm_kernel,
        grid=(n // tm,),
        in_specs=[row, _const_spec((1, D_MODEL))],
        out_specs=row,
        out_shape=jax.ShapeDtypeStruct((n, D_MODEL), F32),
        compiler_params=_params("parallel"),
        name="final_norm",
    )(x, g)


def _row_tile(n, cap):
    t = min(n, cap)
    while n % t:
        t //= 2
    return t


def _tokens_minor(a, lead):
    nd = a.ndim
    perm = tuple(range(lead)) + tuple(range(lead + 1, nd)) + (lead,)
    t = jnp.transpose(a, perm)
    return t.reshape(t.shape[:lead] + (-1, t.shape[-1]))


@jax.jit
def kernel(x_prompt, x_sample, cache_nsa, cache_sb, cache_diff, state_nsa_win, state_pool, page_table,
           norm_mix, w_in, w_out, diff_lambda, diff_norm, pool_w, pool_scale, norm_ffn,
           ffn_gate, ffn_up, ffn_down, router_w, moe_gate, moe_up, moe_down, norm_final):
    b, s, d = x_prompt.shape
    db, t, _ = x_sample.shape
    depth, n_pool, page = cache_nsa.shape[:3]
    past = page_table.shape[1] * page
    wb = state_nsa_win.shape[2]
    wkeep = min(WINDOW, s)
    np_, ns_ = b * s, db * t
    tm_p = _row_tile(s, 512)
    tm_s = _row_tile(ns_, 512)

    tabs_p = _rope_tables(jnp.arange(s))
    tabs_s = _rope_tables(past + (jnp.arange(tm_s) % t))
    cache_nsa_t = _tokens_minor(cache_nsa, 2)
    cache_sb_t = _tokens_minor(cache_sb, 2)
    cache_diff_t = _tokens_minor(cache_diff, 2)
    state_win_t = _tokens_minor(state_nsa_win, 2)

    xp = x_prompt.reshape(np_, d)
    xs = x_sample.reshape(ns_, d)
    outs = [[] for _ in range(10)]
    for l in range(depth):
        lam_init = 0.8 - 0.6 * math.exp(-0.3 * l)
        g_mix = norm_mix[l].reshape(1, d)
        w_l = _prep_w_in(w_in[l])
        lam_l = diff_lambda[l]
        gn_l = diff_norm[l].reshape(1, HEAD_DIM)
        wp_l = _pool_block_diag(pool_w[l])
        ps_l = pool_scale[l].reshape(1, GROUP_W)

        qa, rows, win, gate, qb, kvb, qc, kvc, du = _project(xp, g_mix, w_l, tabs_p, tm_p)
        qa_s, rows_s, win_s, gate_s, qb_s, kvb_s, qc_s, kvc_s, du_s = _project(xs, g_mix, w_l, tabs_s, tm_s)

        p3 = lambda a: a.reshape(b, s, a.shape[-1])
        s3 = lambda a: a.reshape(db, t, a.shape[-1])
        oa = _nsa_prompt_t(p3(qa), p3(gate), p3(rows), p3(win))
        ob = _sb_prompt(p3(qb), p3(kvb))
        oc = _diff_prompt(p3(qc), p3(kvc), lam_l, gn_l, lam_init)
        od = _pool(p3(du), wp_l, ps_l, 0, 0)

        oa_s = _nsa_sample(page_table, s3(qa_s), s3(gate_s), s3(rows_s), s3(win_s), state_win_t,
                           cache_nsa_t, l)
        ob_s = _sb_sample(page_table, s3(qb_s), s3(kvb_s), cache_sb_t, l)
        oc_s = _diff_sample(page_table, s3(qc_s), s3(kvc_s), lam_l, gn_l, cache_diff_t, l, lam_init)
        u_ext = jnp.concatenate([state_pool[l], s3(du_s)], axis=1)
        od_s = _pool(u_ext, wp_l, ps_l, POOL_HIST, past)

        outs[0].append(rows.reshape(b, s, 4, HEAD_DIM))
        outs[1].append(rows_s.reshape(db, t, 4, HEAD_DIM))
        outs[2].append(kvb.reshape(b, s, 2, N_HEADS, HEAD_DIM))
        outs[3].append(kvb_s.reshape(db, t, 2, N_HEADS, HEAD_DIM))
        outs[4].append(kvc.reshape(b, s, 2, N_HEADS, HEAD_DIM))
        outs[5].append(kvc_s.reshape(db, t, 2, N_HEADS, HEAD_DIM))
        outs[6].append(win.reshape(b, s, 2, HEAD_DIM)[:, s - wkeep:])
        outs[7].append(jnp.concatenate([state_nsa_win[l], win_s.reshape(db, t, 2, HEAD_DIM)], axis=1)[:, -wb:])
        outs[8].append(p3(du)[:, s - POOL_HIST:])
        outs[9].append(u_ext[:, -POOL_HIST:])

        o4_p = [a.reshape(np_, GROUP_W) for a in (oa, ob, oc, od)]
        o4_s = [a.reshape(ns_, GROUP_W) for a in (oa_s, ob_s, oc_s, od_s)]
        wo = w_out[l].astype(BF16)
        g_ffn = norm_ffn[l].reshape(1, d)
        i = l // 2
        if l % 2 == 0:
            wg, wu, wd = ffn_gate[i].astype(BF16), ffn_up[i].astype(BF16), ffn_down[i].astype(BF16)
            xp = _ffn(xp, o4_p, wo, g_ffn, wg, wu, wd, tm_p)
            xs = _ffn(xs, o4_s, wo, g_ffn, wg, wu, wd, tm_s)
        else:
            wr = jnp.pad(router_w[i], ((0, 0), (0, LANES - N_EXPERTS))).astype(BF16)
            wg, wu, wd = moe_gate[i].astype(BF16), moe_up[i].astype(BF16), moe_down[i].astype(BF16)
            xp = _moe(xp, o4_p, wo, g_ffn, wr, wg, wu, wd, tm_p)
            xs = _moe(xs, o4_s, wo, g_ffn, wr, wg, wu, wd, tm_s)

    g_fin = norm_final.reshape(1, d)
    y_p = _final_norm(xp, g_fin, tm_p).reshape(b, s, d)
    y_s = _final_norm(xs, g_fin, tm_s).reshape(db, t, d)
    return (y_p, y_s) + tuple(jnp.stack(o) for o in outs)
```

```python
import functools
import math

import numpy as np
import jax
import jax.numpy as jnp
from jax import lax
from jax.experimental import pallas as pl
from jax.experimental.pallas import tpu as pltpu

F32 = jnp.float32
BF16 = jnp.bfloat16
I32 = jnp.int32

D_MODEL = 1024
HEAD_DIM = 64
GROUP_W = D_MODEL // 4
N_HEADS = GROUP_W // HEAD_DIM
DIFF_DQ = HEAD_DIM // 2
ROPE_THETA = 500000.0
CMP_BLOCK = 32
SEL_BLOCK = 64
SEL_TOPK = 8
WINDOW = 512
FORCE_SCORE = 1.0e4
POOL_WINDOWS = (2, 4, 8, 16)
POOL_GW = GROUP_W // len(POOL_WINDOWS)
POOL_HIST = max(POOL_WINDOWS) - 1
N_EXPERTS = 8
NORM_EPS = 1e-6
PROJ_SIZES = (256, 64, 64, 64, 64, 64, 64, 12, 256, 256, 256, 256, 256, 256, 256)

QBLOCK = 128
SB_BLOCK = 256
N_EXTENTS = 8
LANES = 128
SUBLANES = 8
VMEM_LIMIT = 56 * 1024 * 1024

C_AQ, C_ROWS, C_WIN, C_GATE, C_BQ, C_BKV, C_CQ, C_CKV, C_DU, C_END = (
    0, 256, 512, 640, 768, 1024, 1536, 1792, 2304, 2560)

NEG_INF = float("-inf")
QK_SCALE = HEAD_DIM ** -0.5
DIFF_SCALE = DIFF_DQ ** -0.5
EXP_UNDERFLOW = -104.0


def _dot(a, b):
    return jnp.dot(a, b, preferred_element_type=F32)


def _dot_nt(a, b):
    return lax.dot_general(a, b, (((1,), (1,)), ((), ())), preferred_element_type=F32)


def _split_bf16(x, parts):
    out = []
    r = x
    for _ in range(parts):
        p = r.astype(BF16)
        out.append(p)
        r = r - p.astype(F32)
    return out


def _dot_split_lhs(x, w, parts):
    acc = None
    for p in _split_bf16(x, parts):
        t = _dot(p, w)
        acc = t if acc is None else acc + t
    return acc


def _dot_split_rhs(w, x, parts):
    acc = None
    for p in _split_bf16(x, parts):
        t = _dot(w, p)
        acc = t if acc is None else acc + t
    return acc


def _iota(shape, dim):
    return lax.broadcasted_iota(I32, shape, dim)


def _later_keys(n):
    return jnp.where(_iota((n, n), 0) > _iota((n, n), 1), 1.0, 0.0).astype(BF16)


def _rmsnorm(x, g):
    ms = jnp.mean(x * x, axis=-1, keepdims=True)
    return x * lax.rsqrt(ms + NORM_EPS) * g


def _softmax_parts(s, mask):
    s = jnp.where(mask, s, NEG_INF)
    m = jnp.max(s, axis=-1, keepdims=True)
    m = jnp.where(m == NEG_INF, 0.0, m)
    e = jnp.where(mask, jnp.exp(s - m), 0.0)
    l = jnp.maximum(jnp.sum(e, axis=-1, keepdims=True), 1e-30)
    return e, l


def _softplus(z):
    return jnp.maximum(z, 0.0) + jnp.log1p(jnp.exp(-jnp.abs(z)))


def _silu(x):
    return x / (1.0 + jnp.exp(-x))


def _params(*sem):
    return pltpu.CompilerParams(dimension_semantics=sem, vmem_limit_bytes=VMEM_LIMIT)


def _const_spec(shape):
    nd = len(shape)
    return pl.BlockSpec(shape, lambda *_: (0,) * nd)


def _rope_tile(f, c, s1, s2, shift):
    return f * c + pltpu.roll(f, shift, 1) * s1 + pltpu.roll(f, LANES - shift, 1) * s2


def _proj_kernel(x_ref, g_ref, w_ref, ca_ref, sa1_ref, sa2_ref, cc_ref, sc1_ref, sc2_ref,
                 qa_ref, rows_ref, win_ref, gate_ref, qb_ref, kvb_ref, qc_ref, kvc_ref, du_ref):
    h = _rmsnorm(x_ref[...], g_ref[...]).astype(BF16)

    def mm(c0, c1):
        return _dot(h, w_ref[:, c0:c1])

    ca, sa1, sa2 = ca_ref[...], sa1_ref[...], sa2_ref[...]
    cc, sc1, sc2 = cc_ref[...], sc1_ref[...], sc2_ref[...]
    first = _iota(ca.shape, 1) < HEAD_DIM
    cb = jnp.where(first, ca, 1.0)
    sb1 = jnp.where(first, sa1, 0.0)
    sb2 = jnp.where(first, sa2, 0.0)
    ha = HEAD_DIM // 8
    hc = DIFF_DQ // 8

    tiles = lambda f: [f[:, t * LANES:(t + 1) * LANES] for t in range(f.shape[1] // LANES)]
    for t, f in enumerate(tiles(mm(C_AQ, C_ROWS))):
        qa_ref[:, t * LANES:(t + 1) * LANES] = (_rope_tile(f, ca, sa1, sa2, ha) * QK_SCALE).astype(BF16)
    for t, f in enumerate(tiles(mm(C_ROWS, C_WIN))):
        rows_ref[:, t * LANES:(t + 1) * LANES] = _rope_tile(f, cb, sb1, sb2, ha)
    f_win, f_gate = tiles(mm(C_WIN, C_BQ))
    win_ref[...] = _rope_tile(f_win, cb, sb1, sb2, ha)
    gate_ref[...] = 1.0 / (1.0 + jnp.exp(-f_gate))
    qb_ref[...] = (mm(C_BQ, C_BKV) * QK_SCALE).astype(BF16)
    kvb_ref[...] = mm(C_BKV, C_CQ)
    for t, f in enumerate(tiles(mm(C_CQ, C_CKV))):
        qc_ref[:, t * LANES:(t + 1) * LANES] = (_rope_tile(f, cc, sc1, sc2, hc) * DIFF_SCALE).astype(BF16)
    for t, f in enumerate(tiles(mm(C_CKV, C_CKV + 2 * LANES))):
        kvc_ref[:, t * LANES:(t + 1) * LANES] = _rope_tile(f, cc, sc1, sc2, hc)
    kvc_ref[:, 2 * LANES:4 * LANES] = mm(C_CKV + 2 * LANES, C_DU)
    du_ref[...] = mm(C_DU, C_END)


def _project(x, g, w, tabs, tm):
    n = x.shape[0]
    p = tabs[0].shape[0]
    nper = p // tm
    row = lambda c: pl.BlockSpec((tm, c), lambda i: (i, 0))
    tab = pl.BlockSpec((tm, LANES), lambda i: (i % nper, 0))
    outs = [(256, BF16), (256, F32), (128, F32), (128, F32), (256, BF16), (512, F32),
            (256, BF16), (512, F32), (256, F32)]
    return pl.pallas_call(
        _proj_kernel,
        grid=(n // tm,),
        in_specs=[row(D_MODEL), _const_spec((1, D_MODEL)), _const_spec((D_MODEL, C_END))] + [tab] * 6,
        out_specs=[row(c) for c, _ in outs],
        out_shape=[jax.ShapeDtypeStruct((n, c), d) for c, d in outs],
        compiler_params=_params("parallel"),
        name="proj",
    )(x, g, w, *tabs)


def _rope_tables(pos):
    pos = pos.astype(F32)[:, None]
    lane = np.arange(LANES)
    out = []
    for width in (HEAD_DIM, DIFF_DQ):
        half = width // 8
        d = lane % width
        inv = ROPE_THETA ** (-jnp.arange(half, dtype=F32) / half)
        ang = pos * inv[None, :]
        cos, sin = jnp.cos(ang), jnp.sin(ang)
        is1 = d < half
        is2 = (d >= half) & (d < 2 * half)
        fidx = np.where(is1, d, np.where(is2, d - half, 0))
        cosl, sinl = cos[:, fidx], sin[:, fidx]
        rot = jnp.asarray(is1 | is2)[None, :]
        out.append(jnp.where(rot, cosl, 1.0))
        out.append(jnp.where(jnp.asarray(is2)[None, :], sinl, 0.0))
        out.append(jnp.where(jnp.asarray(is1)[None, :], -sinl, 0.0))
    return out


def _prep_w_in(w):
    cuts = np.cumsum(PROJ_SIZES)[:-1].tolist()
    (aq, akc, avc, aks, avs, akw, avw, ag, bq, bk, bv, cq, ck, cv, du) = jnp.split(w, cuts, axis=-1)
    agp = jnp.pad(ag, ((0, 0), (0, LANES - ag.shape[1])))
    return jnp.concatenate([aq, akc, avc, aks, avs, akw, avw, agp, bq, bk, bv, cq, ck, cv, du],
                           axis=-1).astype(BF16)


def _select_blocks(imp, qpos, ns):
    w = imp.shape[1]
    blk = _iota((1, w), 1)
    cur = qpos >> 6
    forced = (blk == 0) | (blk == cur) | (blk == cur - 1)
    valid = (blk * SEL_BLOCK <= qpos) & (blk < ns)
    imp = jnp.where(forced, FORCE_SCORE, imp)
    imp = jnp.where(valid, imp, NEG_INF)
    rank = jnp.zeros(imp.shape, F32)
    for i in range(ns):
        ci = imp[:, i:i + 1]
        tie = jnp.where(blk > i, 1.0, 0.0)
        rank = rank + jnp.where(ci > imp, 1.0, jnp.where(ci == imp, tie, 0.0))
    return jnp.where((rank < min(SEL_TOPK, ns)) & (blk < ns), 1.0, 0.0)


def _select_blocks_t(imp, qpos, ns):
    blk = _iota((ns, 1), 0)
    cur = qpos >> 6
    forced = (blk == 0) | (blk == cur) | (blk == cur - 1)
    imp = jnp.where(forced, FORCE_SCORE, imp)
    imp = jnp.where(blk * SEL_BLOCK <= qpos, imp, NEG_INF)
    rank = jnp.zeros(imp.shape, F32)
    for i in range(ns):
        ci = imp[i:i + 1, :]
        tie = jnp.where(blk > i, 1.0, 0.0)
        rank = rank + jnp.where(ci > imp, 1.0, jnp.where(ci == imp, tie, 0.0))
    return jnp.where(rank < min(SEL_TOPK, ns), 1.0, 0.0)


def _eye(n):
    return jnp.where(_iota((n, n), 0) == _iota((n, n), 1), 1.0, 0.0).astype(BF16)


def _transpose_bf16(x):
    return _dot_nt(_eye(x.shape[1]), x).astype(BF16)


def _softmax_keys_major(s):
    m = jnp.max(s, axis=0, keepdims=True)
    e = jnp.exp(s - m)
    return e.astype(BF16), jnp.sum(e, axis=0, keepdims=True)


def _nsa_prompt_t_kernel(q_ref, g_ref, rows_ref, win_ref, o_ref,
                         kc_ref, vct_ref, ks_ref, vst_ref, kw_ref, vwt_ref, ost_ref, *, seq):
    qi = pl.program_id(1)
    s0 = qi * QBLOCK
    nc = seq // CMP_BLOCK
    ns = seq // SEL_BLOCK
    per = (seq // QBLOCK) // N_EXTENTS
    nq = N_HEADS * QBLOCK

    def cmp_block(shape, dim):
        r = _iota(shape, dim)
        return jnp.where(r < ns, 2 * r, 2 * (r - ns) + 1)

    @pl.when(qi == 0)
    def _():
        pm = jnp.where((_iota((nc, seq), 1) >> 5) == cmp_block((nc, seq), 0),
                       1.0 / CMP_BLOCK, 0.0).astype(BF16)
        kcvc = _dot_split_rhs(pm, rows_ref[0, :, 0:LANES], 2)
        kc_ref[...] = kcvc[:, 0:HEAD_DIM].astype(BF16)
        vct_ref[...] = _transpose_bf16(kcvc[:, HEAD_DIM:2 * HEAD_DIM].astype(BF16))
        ks_ref[...] = rows_ref[0, :, 128:192].astype(BF16)
        vst_ref[...] = _transpose_bf16(rows_ref[0, :, 192:256].astype(BF16))
        kw_ref[...] = win_ref[0, :, 0:64].astype(BF16)
        for c in range(seq // QBLOCK):
            vwt_ref[c] = _transpose_bf16(win_ref[0, c * QBLOCK:(c + 1) * QBLOCK, 64:128].astype(BF16))

    q = q_ref[0]
    q4 = jnp.concatenate([q[:, HEAD_DIM * h:HEAD_DIM * (h + 1)] for h in range(N_HEADS)], axis=0)
    qpos = s0 + _iota((1, QBLOCK), 1)
    qpos4 = jnp.concatenate([qpos] * N_HEADS, axis=1)

    cmask = (cmp_block((nc, 1), 0) + 1) * CMP_BLOCK - 1 <= qpos4
    s = jnp.where(cmask, _dot_nt(kc_ref[...], q4), NEG_INF)
    m = jnp.max(s, axis=0, keepdims=True)
    m = jnp.where(m == NEG_INF, 0.0, m)
    e = jnp.where(cmask, jnp.exp(s - m), 0.0)
    pc = e / jnp.maximum(jnp.sum(e, axis=0, keepdims=True), 1e-30)
    oc = _dot(vct_ref[...], pc.astype(BF16))
    pcs = pc[:, 0:QBLOCK]
    for h in range(1, N_HEADS):
        pcs = pcs + pc[:, h * QBLOCK:(h + 1) * QBLOCK]
    sel = _select_blocks_t(pcs[0:ns] + pcs[ns:2 * ns], qpos, ns)

    for j in range(N_EXTENTS):
        ext = (j + 1) * per * QBLOCK

        @pl.when(qi // per == j)
        def _(ext=ext):
            nse = ext // SEL_BLOCK
            expand = jnp.where((_iota((ext, nse), 0) >> 6) == _iota((ext, nse), 1), 1.0, 0.0).astype(BF16)
            selk = _dot(expand, sel[0:nse].astype(BF16))
            live = (selk > 0.5) & (_iota((ext, 1), 0) <= qpos)
            bias = jnp.where(live, 0.0, NEG_INF)
            s = _dot_nt(ks_ref[0:ext, :], q4) + jnp.concatenate([bias] * N_HEADS, axis=1)
            e, l = _softmax_keys_major(s)
            ost_ref[...] = _dot(vst_ref[:, 0:ext], e) / l

    nwc = WINDOW // QBLOCK + 1
    c0 = jnp.maximum(qi - WINDOW // QBLOCK, 0)
    start = pl.multiple_of(c0 * QBLOCK, QBLOCK)
    kwpos = start + _iota((nwc * QBLOCK, 1), 0)
    wbias = jnp.where((kwpos <= qpos) & (kwpos > qpos - WINDOW), 0.0, NEG_INF)
    s = _dot_nt(kw_ref[pl.ds(start, nwc * QBLOCK), :], q4) + jnp.concatenate([wbias] * N_HEADS, axis=1)
    e, l = _softmax_keys_major(s)
    ow = _dot(vwt_ref[c0], e[0:QBLOCK])
    for c in range(1, nwc):
        ow = ow + _dot(vwt_ref[c0 + c], e[c * QBLOCK:(c + 1) * QBLOCK])
    ow = ow / l

    gt = jnp.transpose(g_ref[0])
    outs = []
    for h in range(N_HEADS):
        cs = slice(h * QBLOCK, (h + 1) * QBLOCK)
        outs.append(gt[3 * h:3 * h + 1] * oc[:, cs] + gt[3 * h + 1:3 * h + 2] * ost_ref[:, cs]
                    + gt[3 * h + 2:3 * h + 3] * ow[:, cs])
    out_t = jnp.concatenate(outs, axis=0)
    o_ref[0] = jnp.concatenate([jnp.transpose(out_t[0:LANES]), jnp.transpose(out_t[LANES:2 * LANES])],
                               axis=1).astype(BF16)


def _nsa_prompt_t(q, g, rows, win):
    b, s, _ = q.shape
    qspec = lambda c: pl.BlockSpec((1, QBLOCK, c), lambda i, j: (i, j, 0))
    full = lambda c: pl.BlockSpec((1, s, c), lambda i, j: (i, 0, 0))
    nc = s // CMP_BLOCK
    return pl.pallas_call(
        functools.partial(_nsa_prompt_t_kernel, seq=s),
        grid=(b, s // QBLOCK),
        in_specs=[qspec(256), qspec(128), full(256), full(128)],
        out_specs=qspec(256),
        out_shape=jax.ShapeDtypeStruct((b, s, GROUP_W), BF16),
        scratch_shapes=[pltpu.VMEM((nc, HEAD_DIM), BF16), pltpu.VMEM((HEAD_DIM, nc), BF16),
                        pltpu.VMEM((s, HEAD_DIM), BF16), pltpu.VMEM((HEAD_DIM, s), BF16),
                        pltpu.VMEM((s, HEAD_DIM), BF16), pltpu.VMEM((s // QBLOCK, HEAD_DIM, QBLOCK), BF16),
                        pltpu.VMEM((HEAD_DIM, N_HEADS * QBLOCK), F32)],
        compiler_params=_params("parallel", "arbitrary"),
        name="nsa_prompt",
    )(q, g, rows, win)


def _sb_prompt_kernel(q_ref, kv_ref, o_ref, kb_ref, vb_ref, r_ref, acc_ref):
    qi = pl.program_id(1)
    s0 = qi * SB_BLOCK

    @pl.when(qi == 0)
    def _():
        kb_ref[...] = kv_ref[0, :, 0:GROUP_W].astype(BF16)
        vb_ref[...] = kv_ref[0, :, GROUP_W:2 * GROUP_W].astype(BF16)

    q = q_ref[0]
    r_ref[...] = jnp.zeros(r_ref.shape, F32)
    acc_ref[...] = jnp.zeros(acc_ref.shape, F32)
    later = _later_keys(SB_BLOCK)
    qpos = s0 + _iota((SB_BLOCK, 1), 0)

    def cond(c):
        it, rmax = c
        return (it <= qi) & (rmax > EXP_UNDERFLOW)

    def body(c):
        it, _ = c
        k0 = pl.multiple_of((qi - it) * SB_BLOCK, SB_BLOCK)
        mask = (k0 + _iota((1, SB_BLOCK), 1)) < qpos
        rm = None
        for h in range(N_HEADS):
            cs = slice(HEAD_DIM * h, HEAD_DIM * (h + 1))
            z = _dot_nt(q[:, cs], kb_ref[pl.ds(k0, SB_BLOCK), cs])
            sp = _softplus(z)
            l1m = jnp.where(mask, -sp, 0.0)
            suf = _dot_split_lhs(l1m, later, 2)
            rh = r_ref[h]
            a = jnp.where(mask, jnp.exp(z - sp + suf + jnp.concatenate([rh, rh], axis=1)), 0.0)
            acc_ref[:, cs] += _dot(a.astype(BF16), vb_ref[pl.ds(k0, SB_BLOCK), cs])
            rn = rh + (suf[:, 0:1] + l1m[:, 0:1])
            r_ref[h] = rn
            rm = rn if rm is None else jnp.maximum(rm, rn)
        return it + 1, jnp.max(rm)

    lax.while_loop(cond, body, (jnp.int32(0), jnp.float32(0.0)))
    o_ref[0] = acc_ref[...].astype(BF16)


def _sb_prompt(q, kv):
    b, s, _ = q.shape
    qspec = pl.BlockSpec((1, SB_BLOCK, GROUP_W), lambda i, j: (i, j, 0))
    return pl.pallas_call(
        _sb_prompt_kernel,
        grid=(b, s // SB_BLOCK),
        in_specs=[qspec, pl.BlockSpec((1, s, 2 * GROUP_W), lambda i, j: (i, 0, 0))],
        out_specs=qspec,
        out_shape=jax.ShapeDtypeStruct((b, s, GROUP_W), BF16),
        scratch_shapes=[pltpu.VMEM((s, GROUP_W), BF16), pltpu.VMEM((s, GROUP_W), BF16),
                        pltpu.VMEM((N_HEADS, SB_BLOCK, LANES), F32), pltpu.VMEM((SB_BLOCK, GROUP_W), F32)],
        compiler_params=_params("parallel", "arbitrary"),
        name="sb_prompt",
    )(q, kv)


def _diff_lambda(lam_ref, lam_init):
    lv = lam_ref[...]
    a = jnp.sum(lv[0:1] * lv[1:2], axis=1, keepdims=True)
    b = jnp.sum(lv[2:3] * lv[3:4], axis=1, keepdims=True)
    return jnp.exp(a) - jnp.exp(b) + lam_init


def _diff_finish(o1, o2, lam, gn, lam_init):
    o = o1 - lam * o2
    return _rmsnorm(o, gn) * (1.0 - lam_init)


def _diff_prompt_kernel(q_ref, kv_ref, lam_ref, gn_ref, o_ref, kb_ref, vb_ref, *, seq, lam_init):
    qi = pl.program_id(1)
    s0 = qi * QBLOCK
    per = (seq // QBLOCK) // N_EXTENTS

    @pl.when(qi == 0)
    def _():
        kb_ref[...] = kv_ref[0, :, 0:GROUP_W].astype(BF16)
        vb_ref[...] = kv_ref[0, :, GROUP_W:2 * GROUP_W].astype(BF16)

    lam = _diff_lambda(lam_ref, lam_init)
    gn = gn_ref[...]
    q = q_ref[0]
    qpos = s0 + _iota((QBLOCK, 1), 0)

    for j in range(N_EXTENTS):
        ext = (j + 1) * per * QBLOCK

        @pl.when(qi // per == j)
        def _(ext=ext):
            bias = jnp.where(_iota((1, ext), 1) <= qpos, 0.0, NEG_INF)
            outs = []
            for h in range(N_HEADS):
                es, ls = [], []
                for m in range(2):
                    cs = slice(HEAD_DIM * h + DIFF_DQ * m, HEAD_DIM * h + DIFF_DQ * (m + 1))
                    s = _dot_nt(q[:, cs], kb_ref[0:ext, cs]) + bias
                    e = jnp.exp(s - jnp.max(s, axis=-1, keepdims=True))
                    es.append(e.astype(BF16))
                    ls.append(jnp.sum(e, axis=-1, keepdims=True))
                o = _dot(jnp.concatenate(es, axis=0), vb_ref[0:ext, HEAD_DIM * h:HEAD_DIM * (h + 1)])
                outs.append(_diff_finish(o[0:QBLOCK] / ls[0], o[QBLOCK:2 * QBLOCK] / ls[1],
                                         lam, gn, lam_init))
            o_ref[0] = jnp.concatenate(outs, axis=1).astype(BF16)


def _diff_prompt(q, kv, lam, gn, lam_init):
    b, s, _ = q.shape
    qspec = pl.BlockSpec((1, QBLOCK, GROUP_W), lambda i, j: (i, j, 0))
    return pl.pallas_call(
        functools.partial(_diff_prompt_kernel, seq=s, lam_init=lam_init),
        grid=(b, s // QBLOCK),
        in_specs=[qspec, pl.BlockSpec((1, s, 2 * GROUP_W), lambda i, j: (i, 0, 0)),
                  pl.BlockSpec((4, DIFF_DQ), lambda i, j: (0, 0)),
                  pl.BlockSpec((1, HEAD_DIM), lambda i, j: (0, 0))],
        out_specs=qspec,
        out_shape=jax.ShapeDtypeStruct((b, s, GROUP_W), BF16),
        scratch_shapes=[pltpu.VMEM((s, GROUP_W), BF16), pltpu.VMEM((s, GROUP_W), BF16)],
        compiler_params=_params("parallel", "arbitrary"),
        name="diff_prompt",
    )(q, kv, lam, gn)


def _pool_kernel(u_ref, w_ref, sc_ref, o_ref, buf_ref, *, hist, pos0):
    n_ext = u_ref.shape[1]
    pad = max(POOL_WINDOWS)
    u = u_ref[0]
    row = _iota((n_ext, 1), 0)
    acc = u
    sums = []
    shift = 1
    for _ in POOL_WINDOWS:
        if n_ext % SUBLANES == 0:
            prev = jnp.where(row >= shift, pltpu.roll(acc, shift, 0), 0.0)
        else:
            buf_ref[0:pad, :] = jnp.zeros((pad, GROUP_W), F32)
            buf_ref[pad:pad + n_ext, :] = acc
            prev = buf_ref[pad - shift:pad - shift + n_ext, :]
        acc = acc + prev
        sums.append(acc)
        shift *= 2
    lane = _iota((1, GROUP_W), 1)
    grp = lane >> 6
    wsum = jnp.where(grp == 0, sums[0], jnp.where(grp == 1, sums[1], jnp.where(grp == 2, sums[2], sums[3])))
    wlen = jnp.where(grp == 0, 2, jnp.where(grp == 1, 4, jnp.where(grp == 2, 8, 16)))
    pos = pos0 - hist + _iota((n_ext, 1), 0)
    cnt = jnp.maximum(jnp.minimum(wlen, pos + 1), 1).astype(F32)
    diff = (wsum / cnt - u)[hist:, :]
    o_ref[0] = (_dot(diff.astype(BF16), w_ref[...]) * sc_ref[...]).astype(BF16)


def _pool(u_ext, w_bd, scale, hist, pos0):
    b, n_ext, _ = u_ext.shape
    n_out = n_ext - hist
    return pl.pallas_call(
        functools.partial(_pool_kernel, hist=hist, pos0=pos0),
        grid=(b,),
        in_specs=[pl.BlockSpec((1, n_ext, GROUP_W), lambda i: (i, 0, 0)),
                  _const_spec((GROUP_W, GROUP_W)), _const_spec((1, GROUP_W))],
        out_specs=pl.BlockSpec((1, n_out, GROUP_W), lambda i: (i, 0, 0)),
        out_shape=jax.ShapeDtypeStruct((b, n_out, GROUP_W), BF16),
        scratch_shapes=[pltpu.VMEM((max(POOL_WINDOWS) + n_ext, GROUP_W), F32)],
        compiler_params=_params("parallel"),
        name="pool",
    )(u_ext, w_bd, scale)


def _pool_block_diag(w):
    z = jnp.zeros((GROUP_W, GROUP_W), F32)
    for gi in range(len(POOL_WINDOWS)):
        z = z.at[gi * POOL_GW:(gi + 1) * POOL_GW, gi * POOL_GW:(gi + 1) * POOL_GW].set(w[gi])
    return z.astype(BF16)


QPAD = SUBLANES


def _pad_rows(x, n):
    return jnp.concatenate([x, jnp.zeros((n - x.shape[0], x.shape[1]), x.dtype)], axis=0)


def _block_diag_q(q, groups, width):
    qp = _pad_rows(q, QPAD)
    qq = jnp.concatenate([qp] * groups, axis=0)
    shape = qq.shape
    shift = int(math.log2(width))
    keep = (_iota(shape, 0) >> 3) == (_iota(shape, 1) >> shift)
    return jnp.where(keep, qq, jnp.zeros_like(qq))


SEQ_PER_STEP = 4


def _page_specs(layer, nb, npg, feat, page):
    return [pl.BlockSpec((1, 1, feat, page), lambda b, pt, i=i, j=j: (layer, pt[b * nb + i, j], 0, 0))
            for i in range(nb) for j in range(npg)]


def _per_sequence(body, nb, kinds, npg):
    def kern(pt_ref, *refs):
        n_in = len(kinds)
        pages = refs[n_in:n_in + nb * npg]
        o_ref = refs[n_in + nb * npg]
        scratch = refs[n_in + nb * npg + 1:]
        live = []
        for i in range(nb):
            ins = []
            for kind, r in zip(kinds, refs[:n_in]):
                if kind == "tok":
                    ins.append(r.at[pl.ds(i, 1)])
                elif kind == "win":
                    ins.append(r.at[:, pl.ds(i, 1)])
                else:
                    ins.append(r)
            live.append(body(pt_ref, *ins, *pages[i * npg:(i + 1) * npg], o_ref.at[pl.ds(i, 1)],
                             *[s.at[i] for s in scratch]))
        done = object()
        while live:
            live = [g for g in live if next(g, done) is not done]
    return kern


def _nsa_sample_kernel(pt_ref, q_ref, g_ref, rn_ref, wn_ref, ws_ref, *rest, past, npg, tq, page):
    pages = rest[:npg]
    o_ref = rest[npg]
    s_ref, e_ref = rest[npg + 1], rest[npg + 2]
    del pt_ref
    rows = N_HEADS * QPAD
    nc = past // CMP_BLOCK
    ns = past // SEL_BLOCK + 1
    cpp = page // CMP_BLOCK
    spp = page // SEL_BLOCK

    qp = _pad_rows(q_ref[0], QPAD)
    q4 = jnp.concatenate([qp[:, HEAD_DIM * h:HEAD_DIM * (h + 1)] for h in range(N_HEADS)], axis=0)
    iq = _iota((rows, 1), 0) & (QPAD - 1)
    qpos = past + iq

    tk = _iota((page, nc), 0)
    cb = _iota((page, nc), 1)
    kcvc = jnp.zeros((2 * HEAD_DIM, nc), F32)
    for p in range(npg):
        pm = jnp.where((tk >> 5) + cpp * p == cb, 1.0 / CMP_BLOCK, 0.0).astype(BF16)
        kcvc = kcvc + _dot_split_lhs(pages[p][0, 0, 0:2 * HEAD_DIM, :], pm, 2)
    yield
    kc = kcvc[0:HEAD_DIM].astype(BF16)
    vc = kcvc[HEAD_DIM:2 * HEAD_DIM].astype(BF16)
    cend = (_iota((1, nc), 1) + 1) * CMP_BLOCK - 1
    e, l = _softmax_parts(_dot(q4, kc), cend <= qpos)
    pc = e / l
    oc = _dot_nt(pc.astype(BF16), vc)
    pcs = pc[0:QPAD]
    for h in range(1, N_HEADS):
        pcs = pcs + pc[h * QPAD:(h + 1) * QPAD]
    pcs = jnp.concatenate([pcs] * N_HEADS, axis=0)
    pair = jnp.where((_iota((nc, LANES), 0) >> 1) == _iota((nc, LANES), 1), 1.0, 0.0).astype(BF16)
    imp = _dot_split_lhs(pcs, pair, 3)
    sel = _select_blocks(imp, qpos, ns)

    yield
    lane = _iota((1, page), 1)
    m = jnp.full((rows, 1), NEG_INF, F32)
    for p in range(npg):
        s = _dot(q4, pages[p][0, 0, 2 * HEAD_DIM:3 * HEAD_DIM, :].astype(BF16))
        selk = sel[:, spp * p:spp * p + 1]
        for j in range(1, spp):
            selk = jnp.where(lane >= j * SEL_BLOCK, sel[:, spp * p + j:spp * p + j + 1], selk)
        s = jnp.where(selk > 0.5, s, NEG_INF)
        s_ref[p * rows:(p + 1) * rows, :] = s
        m = jnp.maximum(m, jnp.max(s, axis=-1, keepdims=True))
    yield
    rn = _pad_rows(rn_ref[0], QPAD)
    jn = _iota((1, QPAD), 1)
    nmask = (jn <= iq) & (jn < tq) & (sel[:, ns - 1:ns] > 0.5)
    sn = jnp.where(nmask, _dot_nt(q4, rn[:, 128:192].astype(BF16)), NEG_INF)
    m = jnp.maximum(m, jnp.max(sn, axis=-1, keepdims=True))
    m = jnp.where(m == NEG_INF, 0.0, m)
    en = jnp.exp(sn - m)
    l = jnp.sum(en, axis=-1, keepdims=True)
    acc = _dot(en.astype(BF16), rn[:, 192:256].astype(BF16))
    e = jnp.exp(s_ref[...] - jnp.concatenate([m] * npg, axis=0))
    e_ref[...] = e.astype(BF16)
    lsum = jnp.sum(e, axis=-1, keepdims=True)
    for p in range(npg):
        l = l + lsum[p * rows:(p + 1) * rows]
        acc = acc + _dot_nt(e_ref[p * rows:(p + 1) * rows, :],
                            pages[p][0, 0, 3 * HEAD_DIM:4 * HEAD_DIM, :].astype(BF16))
    o_s = acc / jnp.maximum(l, 1e-30)

    yield
    ws = ws_ref[0, 0]
    wb = ws.shape[1]
    wn = _pad_rows(wn_ref[0], QPAD)
    kwpos = past - wb + _iota((1, wb), 1)
    wmask = (kwpos <= qpos) & (kwpos > qpos - WINDOW) & (kwpos >= 0)
    s1 = jnp.where(wmask, _dot(q4, ws[0:HEAD_DIM, :].astype(BF16)), NEG_INF)
    s2 = jnp.where((jn <= iq) & (jn < tq), _dot_nt(q4, wn[:, 0:HEAD_DIM].astype(BF16)), NEG_INF)
    m = jnp.maximum(jnp.max(s1, axis=-1, keepdims=True), jnp.max(s2, axis=-1, keepdims=True))
    m = jnp.where(m == NEG_INF, 0.0, m)
    e1 = jnp.exp(s1 - m)
    e2 = jnp.exp(s2 - m)
    l = jnp.sum(e1, axis=-1, keepdims=True) + jnp.sum(e2, axis=-1, keepdims=True)
    o_w = (_dot_nt(e1.astype(BF16), ws[HEAD_DIM:2 * HEAD_DIM, :].astype(BF16))
           + _dot(e2.astype(BF16), wn[:, HEAD_DIM:2 * HEAD_DIM].astype(BF16))) / jnp.maximum(l, 1e-30)

    g = _pad_rows(g_ref[0], QPAD)
    outs = []
    for h in range(N_HEADS):
        rs = slice(h * QPAD, (h + 1) * QPAD)
        outs.append(g[:, 3 * h:3 * h + 1] * oc[rs] + g[:, 3 * h + 1:3 * h + 2] * o_s[rs]
                    + g[:, 3 * h + 2:3 * h + 3] * o_w[rs])
    o_ref[0] = jnp.concatenate(outs, axis=1)[0:tq].astype(BF16)


def _nsa_sample(page_table, q, g, rows_new, win_new, win_state, cache, layer):
    db, tq, _ = q.shape
    npg = page_table.shape[1]
    page = cache.shape[3]
    wb = win_state.shape[3]
    rows = N_HEADS * QPAD
    nb = math.gcd(db, SEQ_PER_STEP)
    tok = lambda c: pl.BlockSpec((nb, tq, c), lambda b, pt: (b, 0, 0))
    grid_spec = pltpu.PrefetchScalarGridSpec(
        num_scalar_prefetch=1,
        grid=(db // nb,),
        in_specs=[tok(256), tok(128), tok(256), tok(128),
                  pl.BlockSpec((1, nb, 2 * HEAD_DIM, wb), lambda b, pt: (layer, b, 0, 0))]
        + _page_specs(layer, nb, npg, 4 * HEAD_DIM, page),
        out_specs=tok(256),
        scratch_shapes=[pltpu.VMEM((nb, npg * rows, page), F32), pltpu.VMEM((nb, npg * rows, page), BF16)],
    )
    body = functools.partial(_nsa_sample_kernel, past=npg * page, npg=npg, tq=tq, page=page)
    return pl.pallas_call(
        _per_sequence(body, nb, ("tok", "tok", "tok", "tok", "win"), npg),
        grid_spec=grid_spec,
        out_shape=jax.ShapeDtypeStruct((db, tq, GROUP_W), BF16),
        compiler_params=_params("arbitrary"),
        name="nsa_sample",
    )(page_table, q, g, rows_new, win_new, win_state, *([cache] * (nb * npg)))


def _sb_sample_kernel(pt_ref, q_ref, kvn_ref, *rest, npg, tq, page):
    pages = rest[:npg]
    o_ref = rest[npg]
    z_ref, a_ref = rest[npg + 1], rest[npg + 2]
    del pt_ref
    rows = N_HEADS * QPAD
    qbd = _block_diag_q(q_ref[0], N_HEADS, HEAD_DIM)
    iq = _iota((rows, 1), 0) & (QPAD - 1)

    for p in range(npg):
        z_ref[p * rows:(p + 1) * rows, :] = _dot(qbd, pages[p][0, 0, 0:GROUP_W, :].astype(BF16))
    yield
    z = z_ref[...]
    sp = _softplus(z)
    l1m = -sp
    suf = _dot_split_lhs(l1m, _later_keys(page), 2)
    tot = suf[:, 0:1] + l1m[:, 0:1]

    yield
    kvn = _pad_rows(kvn_ref[0], QPAD)
    jn = _iota((1, QPAD), 1)
    nmask = (jn < iq) & (jn < tq)
    zn = _dot_nt(qbd, kvn[:, 0:GROUP_W].astype(BF16))
    spn = _softplus(zn)
    l1n = jnp.where(nmask, -spn, 0.0)
    sufn = _dot_split_lhs(l1n, _later_keys(QPAD), 2)
    an = jnp.where(nmask, jnp.exp(zn - spn + sufn), 0.0)
    acc = _dot(an.astype(BF16), kvn[:, GROUP_W:2 * GROUP_W].astype(BF16))

    run = jnp.sum(l1n, axis=-1, keepdims=True)
    carries = [None] * npg
    for p in range(npg - 1, -1, -1):
        carries[p] = run
        run = run + tot[p * rows:(p + 1) * rows]
    a_ref[...] = jnp.exp(z - sp + suf + jnp.concatenate(carries, axis=0)).astype(BF16)
    yield
    for p in range(npg):
        acc = acc + _dot_nt(a_ref[p * rows:(p + 1) * rows, :],
                            pages[p][0, 0, GROUP_W:2 * GROUP_W, :].astype(BF16))

    keep = (_iota(acc.shape, 0) >> 3) == (_iota(acc.shape, 1) >> 6)
    acc = jnp.where(keep, acc, 0.0)
    out = acc[0:QPAD]
    for h in range(1, N_HEADS):
        out = out + acc[h * QPAD:(h + 1) * QPAD]
    o_ref[0] = out[0:tq].astype(BF16)


def _sb_sample(page_table, q, kv_new, cache, layer):
    db, tq, _ = q.shape
    npg = page_table.shape[1]
    page = cache.shape[3]
    rows = N_HEADS * QPAD
    nb = math.gcd(db, SEQ_PER_STEP)
    tok = lambda c: pl.BlockSpec((nb, tq, c), lambda b, pt: (b, 0, 0))
    grid_spec = pltpu.PrefetchScalarGridSpec(
        num_scalar_prefetch=1,
        grid=(db // nb,),
        in_specs=[tok(256), tok(512)] + _page_specs(layer, nb, npg, 2 * GROUP_W, page),
        out_specs=tok(256),
        scratch_shapes=[pltpu.VMEM((nb, npg * rows, page), F32), pltpu.VMEM((nb, npg * rows, page), BF16)],
    )
    body = functools.partial(_sb_sample_kernel, npg=npg, tq=tq, page=page)
    return pl.pallas_call(
        _per_sequence(body, nb, ("tok", "tok"), npg),
        grid_spec=grid_spec,
        out_shape=jax.ShapeDtypeStruct((db, tq, GROUP_W), BF16),
        compiler_params=_params("arbitrary"),
        name="sb_sample",
    )(page_table, q, kv_new, *([cache] * (nb * npg)))


def _diff_sample_kernel(pt_ref, q_ref, kvn_ref, lam_ref, gn_ref, *rest, npg, tq, page, lam_init):
    pages = rest[:npg]
    o_ref = rest[npg]
    s_ref, e_ref = rest[npg + 1], rest[npg + 2]
    del pt_ref
    groups = 2 * N_HEADS
    rows = groups * QPAD
    qbd = _block_diag_q(q_ref[0], groups, DIFF_DQ)
    iq = _iota((rows, 1), 0) & (QPAD - 1)

    m = jnp.full((rows, 1), NEG_INF, F32)
    for p in range(npg):
        s = _dot(qbd, pages[p][0, 0, 0:GROUP_W, :].astype(BF16))
        s_ref[p * rows:(p + 1) * rows, :] = s
        m = jnp.maximum(m, jnp.max(s, axis=-1, keepdims=True))
    yield
    kvn = _pad_rows(kvn_ref[0], QPAD)
    jn = _iota((1, QPAD), 1)
    nmask = (jn <= iq) & (jn < tq)
    sn = jnp.where(nmask, _dot_nt(qbd, kvn[:, 0:GROUP_W].astype(BF16)), NEG_INF)
    m = jnp.maximum(m, jnp.max(sn, axis=-1, keepdims=True))
    en = jnp.exp(sn - m)
    l = jnp.sum(en, axis=-1, keepdims=True)
    acc = _dot(en.astype(BF16), kvn[:, GROUP_W:2 * GROUP_W].astype(BF16))
    e = jnp.exp(s_ref[...] - jnp.concatenate([m] * npg, axis=0))
    e_ref[...] = e.astype(BF16)
    lsum = jnp.sum(e, axis=-1, keepdims=True)
    for p in range(npg):
        l = l + lsum[p * rows:(p + 1) * rows]
        acc = acc + _dot_nt(e_ref[p * rows:(p + 1) * rows, :],
                            pages[p][0, 0, GROUP_W:2 * GROUP_W, :].astype(BF16))
    o = acc / jnp.maximum(l, 1e-30)

    yield
    lam = _diff_lambda(lam_ref, lam_init)
    gn = gn_ref[...]
    outs = []
    for h in range(N_HEADS):
        cs = slice(HEAD_DIM * h, HEAD_DIM * (h + 1))
        o1 = o[(2 * h) * QPAD:(2 * h + 1) * QPAD, cs]
        o2 = o[(2 * h + 1) * QPAD:(2 * h + 2) * QPAD, cs]
        outs.append(_diff_finish(o1, o2, lam, gn, lam_init))
    o_ref[0] = jnp.concatenate(outs, axis=1)[0:tq].astype(BF16)


def _diff_sample(page_table, q, kv_new, lam, gn, cache, layer, lam_init):
    db, tq, _ = q.shape
    npg = page_table.shape[1]
    page = cache.shape[3]
    rows = 2 * N_HEADS * QPAD
    nb = math.gcd(db, SEQ_PER_STEP)
    tok = lambda c: pl.BlockSpec((nb, tq, c), lambda b, pt: (b, 0, 0))
    grid_spec = pltpu.PrefetchScalarGridSpec(
        num_scalar_prefetch=1,
        grid=(db // nb,),
        in_specs=[tok(256), tok(512),
                  pl.BlockSpec((4, DIFF_DQ), lambda b, pt: (0, 0)),
                  pl.BlockSpec((1, HEAD_DIM), lambda b, pt: (0, 0))]
        + _page_specs(layer, nb, npg, 2 * GROUP_W, page),
        out_specs=tok(256),
        scratch_shapes=[pltpu.VMEM((nb, npg * rows, page), F32), pltpu.VMEM((nb, npg * rows, page), BF16)],
    )
    body = functools.partial(_diff_sample_kernel, npg=npg, tq=tq, page=page, lam_init=lam_init)
    return pl.pallas_call(
        _per_sequence(body, nb, ("tok", "tok", "const", "const"), npg),
        grid_spec=grid_spec,
        out_shape=jax.ShapeDtypeStruct((db, tq, GROUP_W), BF16),
        compiler_params=_params("arbitrary"),
        name="diff_sample",
    )(page_table, q, kv_new, lam, gn, *([cache] * (nb * npg)))


def _mix_residual(x_ref, oa_ref, ob_ref, oc_ref, od_ref, wo_ref):
    mixed = jnp.concatenate([oa_ref[...], ob_ref[...], oc_ref[...], od_ref[...]], axis=1)
    return x_ref[...] + _dot(mixed, wo_ref[...])


def _swiglu_act(h, wg_ref, wu_ref, act_ref, width, chunk, lead=()):
    for c0 in range(0, width, chunk):
        c1 = min(c0 + chunk, width)
        idx = lead + (slice(None), slice(c0, c1))
        gate = _dot(h, wg_ref[idx])
        up = _dot(h, wu_ref[idx])
        act_ref[:, c0:c1] = (_silu(gate) * up).astype(BF16)


def _ffn_kernel(x_ref, oa_ref, ob_ref, oc_ref, od_ref, wo_ref, g_ref, wg_ref, wu_ref, wd_ref,
                o_ref, act_ref):
    x = _mix_residual(x_ref, oa_ref, ob_ref, oc_ref, od_ref, wo_ref)
    h = _rmsnorm(x, g_ref[...]).astype(BF16)
    _swiglu_act(h, wg_ref, wu_ref, act_ref, wg_ref.shape[1], 512)
    o_ref[...] = x + _dot(act_ref[...], wd_ref[...])


def _ffn(x, o4, wo, g, wg, wu, wd, tm):
    n = x.shape[0]
    f = wg.shape[1]
    row = lambda c: pl.BlockSpec((tm, c), lambda i: (i, 0))
    return pl.pallas_call(
        _ffn_kernel,
        grid=(n // tm,),
        in_specs=[row(D_MODEL)] + [row(GROUP_W)] * 4
        + [_const_spec((D_MODEL, D_MODEL)), _const_spec((1, D_MODEL)),
           _const_spec((D_MODEL, f)), _const_spec((D_MODEL, f)), _const_spec((f, D_MODEL))],
        out_specs=row(D_MODEL),
        out_shape=jax.ShapeDtypeStruct((n, D_MODEL), F32),
        scratch_shapes=[pltpu.VMEM((tm, f), BF16)],
        compiler_params=_params("parallel"),
        name="ffn",
    )(x, *o4, wo, g, wg, wu, wd)


def _moe_kernel(x_ref, oa_ref, ob_ref, oc_ref, od_ref, wo_ref, g_ref, wr_ref, wg_ref, wu_ref, wd_ref,
                o_ref, xs_ref, h_ref, gate_ref, acc_ref, act_ref):
    e = pl.program_id(1)
    lane = _iota((1, LANES), 1)

    @pl.when(e == 0)
    def _():
        x = _mix_residual(x_ref, oa_ref, ob_ref, oc_ref, od_ref, wo_ref)
        xs_ref[...] = x
        h = _rmsnorm(x, g_ref[...]).astype(BF16)
        h_ref[...] = h
        lanef = lane.astype(F32)
        logits = jnp.where(lane < N_EXPERTS, _dot(h, wr_ref[...]), NEG_INF)
        v1 = jnp.max(logits, axis=-1, keepdims=True)
        i1 = jnp.min(jnp.where(logits == v1, lanef, float(LANES)), axis=-1, keepdims=True)
        rest = jnp.where(lanef == i1, NEG_INF, logits)
        v2 = jnp.max(rest, axis=-1, keepdims=True)
        i2 = jnp.min(jnp.where(rest == v2, lanef, float(LANES)), axis=-1, keepdims=True)
        t = jnp.exp(v2 - v1)
        gate_ref[...] = (jnp.where(lanef == i1, 1.0 / (1.0 + t), 0.0)
                         + jnp.where(lanef == i2, t / (1.0 + t), 0.0))
        acc_ref[...] = jnp.zeros(acc_ref.shape, F32)

    gcol = jnp.sum(jnp.where(lane == e, gate_ref[...], 0.0), axis=-1, keepdims=True)
    _swiglu_act(h_ref[...], wg_ref, wu_ref, act_ref, wg_ref.shape[2], 512, lead=(0,))
    acc_ref[...] += gcol * _dot(act_ref[...], wd_ref[0])

    @pl.when(e == pl.num_programs(1) - 1)
    def _():
        o_ref[...] = xs_ref[...] + acc_ref[...]


def _moe(x, o4, wo, g, wr, wg, wu, wd, tm):
    n = x.shape[0]
    ne, _, f = wg.shape
    row = lambda c: pl.BlockSpec((tm, c), lambda i, e: (i, 0))
    const = lambda shape: pl.BlockSpec(shape, lambda i, e: (0,) * len(shape))
    return pl.pallas_call(
        _moe_kernel,
        grid=(n // tm, ne),
        in_specs=[row(D_MODEL)] + [row(GROUP_W)] * 4
        + [const((D_MODEL, D_MODEL)), const((1, D_MODEL)), const((D_MODEL, LANES)),
           pl.BlockSpec((1, D_MODEL, f), lambda i, e: (e, 0, 0)),
           pl.BlockSpec((1, D_MODEL, f), lambda i, e: (e, 0, 0)),
           pl.BlockSpec((1, f, D_MODEL), lambda i, e: (e, 0, 0))],
        out_specs=row(D_MODEL),
        out_shape=jax.ShapeDtypeStruct((n, D_MODEL), F32),
        scratch_shapes=[pltpu.VMEM((tm, D_MODEL), F32), pltpu.VMEM((tm, D_MODEL), BF16),
                        pltpu.VMEM((tm, LANES), F32), pltpu.VMEM((tm, D_MODEL), F32),
                        pltpu.VMEM((tm, f), BF16)],
        compiler_params=_params("parallel", "arbitrary"),
        name="moe",
    )(x, *o4, wo, g, wr, wg, wu, wd)


def _final_norm_kernel(x_ref, g_ref, o_ref):
    o_ref[...] = _rmsnorm(x_ref[...], g_ref[...])


def _final_norm(x, g, tm):
    n = x.shape[0]
    row = pl.BlockSpec((tm, D_MODEL), lambda i: (i, 0))
    return pl.pallas_call(
        _final_norm_kernel,
        grid=(n // tm,),
        in_specs=[row, _const_spec((1, D_MODEL))],
        out_specs=row,
        out_shape=jax.ShapeDtypeStruct((n, D_MODEL), F32),
        compiler_params=_params("parallel"),
        name="final_norm",
    )(x, g)


def _row_tile(n, cap):
    t = min(n, cap)
    while n % t:
        t //= 2
    return t


def _tokens_minor(a, lead):
    nd = a.ndim
    perm = tuple(range(lead)) + tuple(range(lead + 1, nd)) + (lead,)
    t = jnp.transpose(a, perm)
    return t.reshape(t.shape[:lead] + (-1, t.shape[-1]))


@jax.jit
def kernel(x_prompt, x_sample, cache_nsa, cache_sb, cache_diff, state_nsa_win, state_pool, page_table,
           norm_mix, w_in, w_out, diff_lambda, diff_norm, pool_w, pool_scale, norm_ffn,
           ffn_gate, ffn_up, ffn_down, router_w, moe_gate, moe_up, moe_down, norm_final):
    b, s, d = x_prompt.shape
    db, t, _ = x_sample.shape
    depth, n_pool, page = cache_nsa.shape[:3]
    past = page_table.shape[1] * page
    wb = state_nsa_win.shape[2]
    wkeep = min(WINDOW, s)
    np_, ns_ = b * s, db * t
    tm_p = _row_tile(s, 512)
    tm_s = _row_tile(ns_, 512)

    tabs_p = _rope_tables(jnp.arange(s))
    tabs_s = _rope_tables(past + (jnp.arange(tm_s) % t))
    cache_nsa_t = _tokens_minor(cache_nsa, 2)
    cache_sb_t = _tokens_minor(cache_sb, 2)
    cache_diff_t = _tokens_minor(cache_diff, 2)
    state_win_t = _tokens_minor(state_nsa_win, 2)

    xp = x_prompt.reshape(np_, d)
    xs = x_sample.reshape(ns_, d)
    outs = [[] for _ in range(10)]
    for l in range(depth):
        lam_init = 0.8 - 0.6 * math.exp(-0.3 * l)
        g_mix = norm_mix[l].reshape(1, d)
        w_l = _prep_w_in(w_in[l])
        lam_l = diff_lambda[l]
        gn_l = diff_norm[l].reshape(1, HEAD_DIM)
        wp_l = _pool_block_diag(pool_w[l])
        ps_l = pool_scale[l].reshape(1, GROUP_W)

        qa, rows, win, gate, qb, kvb, qc, kvc, du = _project(xp, g_mix, w_l, tabs_p, tm_p)
        qa_s, rows_s, win_s, gate_s, qb_s, kvb_s, qc_s, kvc_s, du_s = _project(xs, g_mix, w_l, tabs_s, tm_s)

        p3 = lambda a: a.reshape(b, s, a.shape[-1])
        s3 = lambda a: a.reshape(db, t, a.shape[-1])
        oa = _nsa_prompt_t(p3(qa), p3(gate), p3(rows), p3(win))
        ob = _sb_prompt(p3(qb), p3(kvb))
        oc = _diff_prompt(p3(qc), p3(kvc), lam_l, gn_l, lam_init)
        od = _pool(p3(du), wp_l, ps_l, 0, 0)

        oa_s = _nsa_sample(page_table, s3(qa_s), s3(gate_s), s3(rows_s), s3(win_s), state_win_t,
                           cache_nsa_t, l)
        ob_s = _sb_sample(page_table, s3(qb_s), s3(kvb_s), cache_sb_t, l)
        oc_s = _diff_sample(page_table, s3(qc_s), s3(kvc_s), lam_l, gn_l, cache_diff_t, l, lam_init)
        u_ext = jnp.concatenate([state_pool[l], s3(du_s)], axis=1)
        od_s = _pool(u_ext, wp_l, ps_l, POOL_HIST, past)

        outs[0].append(rows.reshape(b, s, 4, HEAD_DIM))
        outs[1].append(rows_s.reshape(db, t, 4, HEAD_DIM))
        outs[2].append(kvb.reshape(b, s, 2, N_HEADS, HEAD_DIM))
        outs[3].append(kvb_s.reshape(db, t, 2, N_HEADS, HEAD_DIM))
        outs[4].append(kvc.reshape(b, s, 2, N_HEADS, HEAD_DIM))
        outs[5].append(kvc_s.reshape(db, t, 2, N_HEADS, HEAD_DIM))
        outs[6].append(win.reshape(b, s, 2, HEAD_DIM)[:, s - wkeep:])
        outs[7].append(jnp.concatenate([state_nsa_win[l], win_s.reshape(db, t, 2, HEAD_DIM)], axis=1)[:, -wb:])
        outs[8].append(p3(du)[:, s - POOL_HIST:])
        outs[9].append(u_ext[:, -POOL_HIST:])

        o4_p = [a.reshape(np_, GROUP_W) for a in (oa, ob, oc, od)]
        o4_s = [a.reshape(ns_, GROUP_W) for a in (oa_s, ob_s, oc_s, od_s)]
        wo = w_out[l].astype(BF16)
        g_ffn = norm_ffn[l].reshape(1, d)
        i = l // 2
        if l % 2 == 0:
            wg, wu, wd = ffn_gate[i].astype(BF16), ffn_up[i].astype(BF16), ffn_down[i].astype(BF16)
            xp = _ffn(xp, o4_p, wo, g_ffn, wg, wu, wd, tm_p)
            xs = _ffn(xs, o4_s, wo, g_ffn, wg, wu, wd, tm_s)
        else:
            wr = jnp.pad(router_w[i], ((0, 0), (0, LANES - N_EXPERTS))).astype(BF16)
            wg, wu, wd = moe_gate[i].astype(BF16), moe_up[i].astype(BF16), moe_down[i].astype(BF16)
            xp = _moe(xp, o4_p, wo, g_ffn, wr, wg, wu, wd, tm_p)
            xs = _moe(xs, o4_s, wo, g_ffn, wr, wg, wu, wd, tm_s)

    g_fin = norm_final.reshape(1, d)
    y_p = _final_norm(xp, g_fin, tm_p).reshape(b, s, d)
    y_s = _final_norm(xs, g_fin, tm_s).reshape(db, t, d)
    return (y_p, y_s) + tuple(jnp.stack(o) for o in outs)
```

```python
import functools
import math

import numpy as np
import jax
import jax.numpy as jnp
from jax import lax
from jax.experimental import pallas as pl
from jax.experimental.pallas import tpu as pltpu

F32 = jnp.float32
BF16 = jnp.bfloat16
I32 = jnp.int32

D_MODEL = 1024
HEAD_DIM = 64
GROUP_W = D_MODEL // 4
N_HEADS = GROUP_W // HEAD_DIM
DIFF_DQ = HEAD_DIM // 2
ROPE_THETA = 500000.0
CMP_BLOCK = 32
SEL_BLOCK = 64
SEL_TOPK = 8
WINDOW = 512
FORCE_SCORE = 1.0e4
POOL_WINDOWS = (2, 4, 8, 16)
POOL_GW = GROUP_W // len(POOL_WINDOWS)
POOL_HIST = max(POOL_WINDOWS) - 1
N_EXPERTS = 8
NORM_EPS = 1e-6
PROJ_SIZES = (256, 64, 64, 64, 64, 64, 64, 12, 256, 256, 256, 256, 256, 256, 256)

QBLOCK = 128
SB_BLOCK = 256
N_EXTENTS = 8
LANES = 128
SUBLANES = 8
VMEM_LIMIT = 56 * 1024 * 1024

C_AQ, C_ROWS, C_WIN, C_GATE, C_BQ, C_BKV, C_CQ, C_CKV, C_DU, C_END = (
    0, 256, 512, 640, 768, 1024, 1536, 1792, 2304, 2560)

NEG_INF = float("-inf")
QK_SCALE = HEAD_DIM ** -0.5
DIFF_SCALE = DIFF_DQ ** -0.5
EXP_UNDERFLOW = -104.0


def _dot(a, b):
    return jnp.dot(a, b, preferred_element_type=F32)


def _dot_nt(a, b):
    return lax.dot_general(a, b, (((1,), (1,)), ((), ())), preferred_element_type=F32)


def _split_bf16(x, parts):
    out = []
    r = x
    for _ in range(parts):
        p = r.astype(BF16)
        out.append(p)
        r = r - p.astype(F32)
    return out


def _dot_split_lhs(x, w, parts):
    acc = None
    for p in _split_bf16(x, parts):
        t = _dot(p, w)
        acc = t if acc is None else acc + t
    return acc


def _dot_split_rhs(w, x, parts):
    acc = None
    for p in _split_bf16(x, parts):
        t = _dot(w, p)
        acc = t if acc is None else acc + t
    return acc


def _iota(shape, dim):
    return lax.broadcasted_iota(I32, shape, dim)


def _later_keys(n):
    return jnp.where(_iota((n, n), 0) > _iota((n, n), 1), 1.0, 0.0).astype(BF16)


def _rmsnorm(x, g):
    ms = jnp.mean(x * x, axis=-1, keepdims=True)
    return x * lax.rsqrt(ms + NORM_EPS) * g


def _softmax_parts(s, mask):
    s = jnp.where(mask, s, NEG_INF)
    m = jnp.max(s, axis=-1, keepdims=True)
    m = jnp.where(m == NEG_INF, 0.0, m)
    e = jnp.where(mask, jnp.exp(s - m), 0.0)
    l = jnp.maximum(jnp.sum(e, axis=-1, keepdims=True), 1e-30)
    return e, l


def _softplus(z):
    return jnp.maximum(z, 0.0) + jnp.log1p(jnp.exp(-jnp.abs(z)))


def _silu(x):
    return x / (1.0 + jnp.exp(-x))


def _params(*sem):
    return pltpu.CompilerParams(dimension_semantics=sem, vmem_limit_bytes=VMEM_LIMIT)


def _const_spec(shape):
    nd = len(shape)
    return pl.BlockSpec(shape, lambda *_: (0,) * nd)


def _rope_tile(f, c, s1, s2, shift):
    return f * c + pltpu.roll(f, shift, 1) * s1 + pltpu.roll(f, LANES - shift, 1) * s2


def _proj_kernel(x_ref, g_ref, w_ref, ca_ref, sa1_ref, sa2_ref, cc_ref, sc1_ref, sc2_ref,
                 qa_ref, rows_ref, win_ref, gate_ref, qb_ref, kvb_ref, qc_ref, kvc_ref, du_ref):
    h = _rmsnorm(x_ref[...], g_ref[...]).astype(BF16)

    def mm(c0, c1):
        return _dot(h, w_ref[:, c0:c1])

    ca, sa1, sa2 = ca_ref[...], sa1_ref[...], sa2_ref[...]
    cc, sc1, sc2 = cc_ref[...], sc1_ref[...], sc2_ref[...]
    first = _iota(ca.shape, 1) < HEAD_DIM
    cb = jnp.where(first, ca, 1.0)
    sb1 = jnp.where(first, sa1, 0.0)
    sb2 = jnp.where(first, sa2, 0.0)
    ha = HEAD_DIM // 8
    hc = DIFF_DQ // 8

    tiles = lambda f: [f[:, t * LANES:(t + 1) * LANES] for t in range(f.shape[1] // LANES)]
    for t, f in enumerate(tiles(mm(C_AQ, C_ROWS))):
        qa_ref[:, t * LANES:(t + 1) * LANES] = (_rope_tile(f, ca, sa1, sa2, ha) * QK_SCALE).astype(BF16)
    for t, f in enumerate(tiles(mm(C_ROWS, C_WIN))):
        rows_ref[:, t * LANES:(t + 1) * LANES] = _rope_tile(f, cb, sb1, sb2, ha)
    f_win, f_gate = tiles(mm(C_WIN, C_BQ))
    win_ref[...] = _rope_tile(f_win, cb, sb1, sb2, ha)
    gate_ref[...] = 1.0 / (1.0 + jnp.exp(-f_gate))
    qb_ref[...] = (mm(C_BQ, C_BKV) * QK_SCALE).astype(BF16)
    kvb_ref[...] = mm(C_BKV, C_CQ)
    for t, f in enumerate(tiles(mm(C_CQ, C_CKV))):
        qc_ref[:, t * LANES:(t + 1) * LANES] = (_rope_tile(f, cc, sc1, sc2, hc) * DIFF_SCALE).astype(BF16)
    for t, f in enumerate(tiles(mm(C_CKV, C_CKV + 2 * LANES))):
        kvc_ref[:, t * LANES:(t + 1) * LANES] = _rope_tile(f, cc, sc1, sc2, hc)
    kvc_ref[:, 2 * LANES:4 * LANES] = mm(C_CKV + 2 * LANES, C_DU)
    du_ref[...] = mm(C_DU, C_END)


def _project(x, g, w, tabs, tm):
    n = x.shape[0]
    p = tabs[0].shape[0]
    nper = p // tm
    row = lambda c: pl.BlockSpec((tm, c), lambda i: (i, 0))
    tab = pl.BlockSpec((tm, LANES), lambda i: (i % nper, 0))
    outs = [(256, BF16), (256, F32), (128, F32), (128, F32), (256, BF16), (512, F32),
            (256, BF16), (512, F32), (256, F32)]
    return pl.pallas_call(
        _proj_kernel,
        grid=(n // tm,),
        in_specs=[row(D_MODEL), _const_spec((1, D_MODEL)), _const_spec((D_MODEL, C_END))] + [tab] * 6,
        out_specs=[row(c) for c, _ in outs],
        out_shape=[jax.ShapeDtypeStruct((n, c), d) for c, d in outs],
        compiler_params=_params("parallel"),
        name="proj",
    )(x, g, w, *tabs)


def _rope_tables(pos):
    pos = pos.astype(F32)[:, None]
    lane = np.arange(LANES)
    out = []
    for width in (HEAD_DIM, DIFF_DQ):
        half = width // 8
        d = lane % width
        inv = ROPE_THETA ** (-jnp.arange(half, dtype=F32) / half)
        ang = pos * inv[None, :]
        cos, sin = jnp.cos(ang), jnp.sin(ang)
        is1 = d < half
        is2 = (d >= half) & (d < 2 * half)
        fidx = np.where(is1, d, np.where(is2, d - half, 0))
        cosl, sinl = cos[:, fidx], sin[:, fidx]
        rot = jnp.asarray(is1 | is2)[None, :]
        out.append(jnp.where(rot, cosl, 1.0))
        out.append(jnp.where(jnp.asarray(is2)[None, :], sinl, 0.0))
        out.append(jnp.where(jnp.asarray(is1)[None, :], -sinl, 0.0))
    return out


def _prep_w_in(w):
    cuts = np.cumsum(PROJ_SIZES)[:-1].tolist()
    (aq, akc, avc, aks, avs, akw, avw, ag, bq, bk, bv, cq, ck, cv, du) = jnp.split(w, cuts, axis=-1)
    agp = jnp.pad(ag, ((0, 0), (0, LANES - ag.shape[1])))
    return jnp.concatenate([aq, akc, avc, aks, avs, akw, avw, agp, bq, bk, bv, cq, ck, cv, du],
                           axis=-1).astype(BF16)


def _select_blocks(imp, qpos, ns):
    w = imp.shape[1]
    blk = _iota((1, w), 1)
    cur = qpos >> 6
    forced = (blk == 0) | (blk == cur) | (blk == cur - 1)
    valid = (blk * SEL_BLOCK <= qpos) & (blk < ns)
    imp = jnp.where(forced, FORCE_SCORE, imp)
    imp = jnp.where(valid, imp, NEG_INF)
    rank = jnp.zeros(imp.shape, F32)
    for i in range(ns):
        ci = imp[:, i:i + 1]
        tie = jnp.where(blk > i, 1.0, 0.0)
        rank = rank + jnp.where(ci > imp, 1.0, jnp.where(ci == imp, tie, 0.0))
    return jnp.where((rank < min(SEL_TOPK, ns)) & (blk < ns), 1.0, 0.0)


def _select_blocks_t(imp, qpos, ns):
    blk = _iota((ns, 1), 0)
    cur = qpos >> 6
    forced = (blk == 0) | (blk == cur) | (blk == cur - 1)
    imp = jnp.where(forced, FORCE_SCORE, imp)
    imp = jnp.where(blk * SEL_BLOCK <= qpos, imp, NEG_INF)
    rank = jnp.zeros(imp.shape, F32)
    for i in range(ns):
        ci = imp[i:i + 1, :]
        tie = jnp.where(blk > i, 1.0, 0.0)
        rank = rank + jnp.where(ci > imp, 1.0, jnp.where(ci == imp, tie, 0.0))
    return jnp.where(rank < min(SEL_TOPK, ns), 1.0, 0.0)


def _eye(n):
    return jnp.where(_iota((n, n), 0) == _iota((n, n), 1), 1.0, 0.0).astype(BF16)


def _transpose_bf16(x):
    return _dot_nt(_eye(x.shape[1]), x).astype(BF16)


def _softmax_keys_major(s):
    m = jnp.max(s, axis=0, keepdims=True)
    e = jnp.exp(s - m)
    return e.astype(BF16), jnp.sum(e, axis=0, keepdims=True)


def _nsa_prompt_t_kernel(q_ref, g_ref, rows_ref, win_ref, o_ref,
                         kc_ref, vct_ref, ks_ref, vst_ref, kw_ref, vwt_ref, ost_ref, *, seq):
    qi = pl.program_id(1)
    s0 = qi * QBLOCK
    nc = seq // CMP_BLOCK
    ns = seq // SEL_BLOCK
    per = (seq // QBLOCK) // N_EXTENTS
    nq = N_HEADS * QBLOCK

    def cmp_block(shape, dim):
        r = _iota(shape, dim)
        return jnp.where(r < ns, 2 * r, 2 * (r - ns) + 1)

    @pl.when(qi == 0)
    def _():
        pm = jnp.where((_iota((nc, seq), 1) >> 5) == cmp_block((nc, seq), 0),
                       1.0 / CMP_BLOCK, 0.0).astype(BF16)
        kcvc = _dot_split_rhs(pm, rows_ref[0, :, 0:LANES], 2)
        kc_ref[...] = kcvc[:, 0:HEAD_DIM].astype(BF16)
        vct_ref[...] = _transpose_bf16(kcvc[:, HEAD_DIM:2 * HEAD_DIM].astype(BF16))
        ks_ref[...] = rows_ref[0, :, 128:192].astype(BF16)
        vst_ref[...] = _transpose_bf16(rows_ref[0, :, 192:256].astype(BF16))
        kw_ref[...] = win_ref[0, :, 0:64].astype(BF16)
        for c in range(seq // QBLOCK):
            vwt_ref[c] = _transpose_bf16(win_ref[0, c * QBLOCK:(c + 1) * QBLOCK, 64:128].astype(BF16))

    q = q_ref[0]
    q4 = jnp.concatenate([q[:, HEAD_DIM * h:HEAD_DIM * (h + 1)] for h in range(N_HEADS)], axis=0)
    qpos = s0 + _iota((1, QBLOCK), 1)
    qpos4 = jnp.concatenate([qpos] * N_HEADS, axis=1)

    cmask = (cmp_block((nc, 1), 0) + 1) * CMP_BLOCK - 1 <= qpos4
    s = jnp.where(cmask, _dot_nt(kc_ref[...], q4), NEG_INF)
    m = jnp.max(s, axis=0, keepdims=True)
    m = jnp.where(m == NEG_INF, 0.0, m)
    e = jnp.where(cmask, jnp.exp(s - m), 0.0)
    pc = e / jnp.maximum(jnp.sum(e, axis=0, keepdims=True), 1e-30)
    oc = _dot(vct_ref[...], pc.astype(BF16))
    pcs = pc[:, 0:QBLOCK]
    for h in range(1, N_HEADS):
        pcs = pcs + pc[:, h * QBLOCK:(h + 1) * QBLOCK]
    sel = _select_blocks_t(pcs[0:ns] + pcs[ns:2 * ns], qpos, ns)

    for j in range(N_EXTENTS):
        ext = (j + 1) * per * QBLOCK

        @pl.when(qi // per == j)
        def _(ext=ext):
            nse = ext // SEL_BLOCK
            expand = jnp.where((_iota((ext, nse), 0) >> 6) == _iota((ext, nse), 1), 1.0, 0.0).astype(BF16)
            selk = _dot(expand, sel[0:nse].astype(BF16))
            live = (selk > 0.5) & (_iota((ext, 1), 0) <= qpos)
            bias = jnp.where(live, 0.0, NEG_INF)
            s = _dot_nt(ks_ref[0:ext, :], q4) + jnp.concatenate([bias] * N_HEADS, axis=1)
            e, l = _softmax_keys_major(s)
            ost_ref[...] = _dot(vst_ref[:, 0:ext], e) / l

    nwc = WINDOW // QBLOCK + 1
    c0 = jnp.maximum(qi - WINDOW // QBLOCK, 0)
    start = pl.multiple_of(c0 * QBLOCK, QBLOCK)
    kwpos = start + _iota((nwc * QBLOCK, 1), 0)
    wbias = jnp.where((kwpos <= qpos) & (kwpos > qpos - WINDOW), 0.0, NEG_INF)
    s = _dot_nt(kw_ref[pl.ds(start, nwc * QBLOCK), :], q4) + jnp.concatenate([wbias] * N_HEADS, axis=1)
    e, l = _softmax_keys_major(s)
    ow = _dot(vwt_ref[c0], e[0:QBLOCK])
    for c in range(1, nwc):
        ow = ow + _dot(vwt_ref[c0 + c], e[c * QBLOCK:(c + 1) * QBLOCK])
    ow = ow / l

    gt = jnp.transpose(g_ref[0])
    outs = []
    for h in range(N_HEADS):
        cs = slice(h * QBLOCK, (h + 1) * QBLOCK)
        outs.append(gt[3 * h:3 * h + 1] * oc[:, cs] + gt[3 * h + 1:3 * h + 2] * ost_ref[:, cs]
                    + gt[3 * h + 2:3 * h + 3] * ow[:, cs])
    out_t = jnp.concatenate(outs, axis=0)
    o_ref[0] = jnp.concatenate([jnp.transpose(out_t[0:LANES]), jnp.transpose(out_t[LANES:2 * LANES])],
                               axis=1).astype(BF16)


def _nsa_prompt_t(q, g, rows, win):
    b, s, _ = q.shape
    qspec = lambda c: pl.BlockSpec((1, QBLOCK, c), lambda i, j: (i, j, 0))
    full = lambda c: pl.BlockSpec((1, s, c), lambda i, j: (i, 0, 0))
    nc = s // CMP_BLOCK
    return pl.pallas_call(
        functools.partial(_nsa_prompt_t_kernel, seq=s),
        grid=(b, s // QBLOCK),
        in_specs=[qspec(256), qspec(128), full(256), full(128)],
        out_specs=qspec(256),
        out_shape=jax.ShapeDtypeStruct((b, s, GROUP_W), BF16),
        scratch_shapes=[pltpu.VMEM((nc, HEAD_DIM), BF16), pltpu.VMEM((HEAD_DIM, nc), BF16),
                        pltpu.VMEM((s, HEAD_DIM), BF16), pltpu.VMEM((HEAD_DIM, s), BF16),
                        pltpu.VMEM((s, HEAD_DIM), BF16), pltpu.VMEM((s // QBLOCK, HEAD_DIM, QBLOCK), BF16),
                        pltpu.VMEM((HEAD_DIM, N_HEADS * QBLOCK), F32)],
        compiler_params=_params("parallel", "arbitrary"),
        name="nsa_prompt",
    )(q, g, rows, win)


def _sb_prompt_kernel(q_ref, kv_ref, o_ref, kb_ref, vb_ref, r_ref, acc_ref):
    qi = pl.program_id(1)
    s0 = qi * SB_BLOCK

    @pl.when(qi == 0)
    def _():
        kb_ref[...] = kv_ref[0, :, 0:GROUP_W].astype(BF16)
        vb_ref[...] = kv_ref[0, :, GROUP_W:2 * GROUP_W].astype(BF16)

    q = q_ref[0]
    r_ref[...] = jnp.zeros(r_ref.shape, F32)
    acc_ref[...] = jnp.zeros(acc_ref.shape, F32)
    later = _later_keys(SB_BLOCK)
    qpos = s0 + _iota((SB_BLOCK, 1), 0)

    def cond(c):
        it, rmax = c
        return (it <= qi) & (rmax > EXP_UNDERFLOW)

    def body(c):
        it, _ = c
        k0 = pl.multiple_of((qi - it) * SB_BLOCK, SB_BLOCK)
        mask = (k0 + _iota((1, SB_BLOCK), 1)) < qpos
        rm = None
        for h in range(N_HEADS):
            cs = slice(HEAD_DIM * h, HEAD_DIM * (h + 1))
            z = _dot_nt(q[:, cs], kb_ref[pl.ds(k0, SB_BLOCK), cs])
            sp = _softplus(z)
            l1m = jnp.where(mask, -sp, 0.0)
            suf = _dot_split_lhs(l1m, later, 2)
            rh = r_ref[h]
            a = jnp.where(mask, jnp.exp(z - sp + suf + jnp.concatenate([rh, rh], axis=1)), 0.0)
            acc_ref[:, cs] += _dot(a.astype(BF16), vb_ref[pl.ds(k0, SB_BLOCK), cs])
            rn = rh + (suf[:, 0:1] + l1m[:, 0:1])
            r_ref[h] = rn
            rm = rn if rm is None else jnp.maximum(rm, rn)
        return it + 1, jnp.max(rm)

    lax.while_loop(cond, body, (jnp.int32(0), jnp.float32(0.0)))
    o_ref[0] = acc_ref[...].astype(BF16)


def _sb_prompt(q, kv):
    b, s, _ = q.shape
    qspec = pl.BlockSpec((1, SB_BLOCK, GROUP_W), lambda i, j: (i, j, 0))
    return pl.pallas_call(
        _sb_prompt_kernel,
        grid=(b, s // SB_BLOCK),
        in_specs=[qspec, pl.BlockSpec((1, s, 2 * GROUP_W), lambda i, j: (i, 0, 0))],
        out_specs=qspec,
        out_shape=jax.ShapeDtypeStruct((b, s, GROUP_W), BF16),
        scratch_shapes=[pltpu.VMEM((s, GROUP_W), BF16), pltpu.VMEM((s, GROUP_W), BF16),
                        pltpu.VMEM((N_HEADS, SB_BLOCK, LANES), F32), pltpu.VMEM((SB_BLOCK, GROUP_W), F32)],
        compiler_params=_params("parallel", "arbitrary"),
        name="sb_prompt",
    )(q, kv)


def _diff_lambda(lam_ref, lam_init):
    lv = lam_ref[...]
    a = jnp.sum(lv[0:1] * lv[1:2], axis=1, keepdims=True)
    b = jnp.sum(lv[2:3] * lv[3:4], axis=1, keepdims=True)
    return jnp.exp(a) - jnp.exp(b) + lam_init


def _diff_finish(o1, o2, lam, gn, lam_init):
    o = o1 - lam * o2
    return _rmsnorm(o, gn) * (1.0 - lam_init)


def _diff_prompt_kernel(q_ref, kv_ref, lam_ref, gn_ref, o_ref, kb_ref, vb_ref, *, seq, lam_init):
    qi = pl.program_id(1)
    s0 = qi * QBLOCK
    per = (seq // QBLOCK) // N_EXTENTS

    @pl.when(qi == 0)
    def _():
        kb_ref[...] = kv_ref[0, :, 0:GROUP_W].astype(BF16)
        vb_ref[...] = kv_ref[0, :, GROUP_W:2 * GROUP_W].astype(BF16)

    lam = _diff_lambda(lam_ref, lam_init)
    gn = gn_ref[...]
    q = q_ref[0]
    qpos = s0 + _iota((QBLOCK, 1), 0)

    for j in range(N_EXTENTS):
        ext = (j + 1) * per * QBLOCK

        @pl.when(qi // per == j)
        def _(ext=ext):
            bias = jnp.where(_iota((1, ext), 1) <= qpos, 0.0, NEG_INF)
            outs = []
            for h in range(N_HEADS):
                es, ls = [], []
                for m in range(2):
                    cs = slice(HEAD_DIM * h + DIFF_DQ * m, HEAD_DIM * h + DIFF_DQ * (m + 1))
                    s = _dot_nt(q[:, cs], kb_ref[0:ext, cs]) + bias
                    e = jnp.exp(s - jnp.max(s, axis=-1, keepdims=True))
                    es.append(e.astype(BF16))
                    ls.append(jnp.sum(e, axis=-1, keepdims=True))
                o = _dot(jnp.concatenate(es, axis=0), vb_ref[0:ext, HEAD_DIM * h:HEAD_DIM * (h + 1)])
                outs.append(_diff_finish(o[0:QBLOCK] / ls[0], o[QBLOCK:2 * QBLOCK] / ls[1],
                                         lam, gn, lam_init))
            o_ref[0] = jnp.concatenate(outs, axis=1).astype(BF16)


def _diff_prompt(q, kv, lam, gn, lam_init):
    b, s, _ = q.shape
    qspec = pl.BlockSpec((1, QBLOCK, GROUP_W), lambda i, j: (i, j, 0))
    return pl.pallas_call(
        functools.partial(_diff_prompt_kernel, seq=s, lam_init=lam_init),
        grid=(b, s // QBLOCK),
        in_specs=[qspec, pl.BlockSpec((1, s, 2 * GROUP_W), lambda i, j: (i, 0, 0)),
                  pl.BlockSpec((4, DIFF_DQ), lambda i, j: (0, 0)),
                  pl.BlockSpec((1, HEAD_DIM), lambda i, j: (0, 0))],
        out_specs=qspec,
        out_shape=jax.ShapeDtypeStruct((b, s, GROUP_W), BF16),
        scratch_shapes=[pltpu.VMEM((s, GROUP_W), BF16), pltpu.VMEM((s, GROUP_W), BF16)],
        compiler_params=_params("parallel", "arbitrary"),
        name="diff_prompt",
    )(q, kv, lam, gn)


def _pool_kernel(u_ref, w_ref, sc_ref, o_ref, buf_ref, *, hist, pos0):
    n_ext = u_ref.shape[1]
    pad = max(POOL_WINDOWS)
    u = u_ref[0]
    row = _iota((n_ext, 1), 0)
    acc = u
    sums = []
    shift = 1
    for _ in POOL_WINDOWS:
        if n_ext % SUBLANES == 0:
            prev = jnp.where(row >= shift, pltpu.roll(acc, shift, 0), 0.0)
        else:
            buf_ref[0:pad, :] = jnp.zeros((pad, GROUP_W), F32)
            buf_ref[pad:pad + n_ext, :] = acc
            prev = buf_ref[pad - shift:pad - shift + n_ext, :]
        acc = acc + prev
        sums.append(acc)
        shift *= 2
    lane = _iota((1, GROUP_W), 1)
    grp = lane >> 6
    wsum = jnp.where(grp == 0, sums[0], jnp.where(grp == 1, sums[1], jnp.where(grp == 2, sums[2], sums[3])))
    wlen = jnp.where(grp == 0, 2, jnp.where(grp == 1, 4, jnp.where(grp == 2, 8, 16)))
    pos = pos0 - hist + _iota((n_ext, 1), 0)
    cnt = jnp.maximum(jnp.minimum(wlen, pos + 1), 1).astype(F32)
    diff = (wsum / cnt - u)[hist:, :]
    o_ref[0] = (_dot(diff.astype(BF16), w_ref[...]) * sc_ref[...]).astype(BF16)


def _pool(u_ext, w_bd, scale, hist, pos0):
    b, n_ext, _ = u_ext.shape
    n_out = n_ext - hist
    return pl.pallas_call(
        functools.partial(_pool_kernel, hist=hist, pos0=pos0),
        grid=(b,),
        in_specs=[pl.BlockSpec((1, n_ext, GROUP_W), lambda i: (i, 0, 0)),
                  _const_spec((GROUP_W, GROUP_W)), _const_spec((1, GROUP_W))],
        out_specs=pl.BlockSpec((1, n_out, GROUP_W), lambda i: (i, 0, 0)),
        out_shape=jax.ShapeDtypeStruct((b, n_out, GROUP_W), BF16),
        scratch_shapes=[pltpu.VMEM((max(POOL_WINDOWS) + n_ext, GROUP_W), F32)],
        compiler_params=_params("parallel"),
        name="pool",
    )(u_ext, w_bd, scale)


def _pool_block_diag(w):
    z = jnp.zeros((GROUP_W, GROUP_W), F32)
    for gi in range(len(POOL_WINDOWS)):
        z = z.at[gi * POOL_GW:(gi + 1) * POOL_GW, gi * POOL_GW:(gi + 1) * POOL_GW].set(w[gi])
    return z.astype(BF16)


QPAD = SUBLANES


def _pad_rows(x, n):
    return jnp.concatenate([x, jnp.zeros((n - x.shape[0], x.shape[1]), x.dtype)], axis=0)


def _block_diag_q(q, groups, width):
    qp = _pad_rows(q, QPAD)
    qq = jnp.concatenate([qp] * groups, axis=0)
    shape = qq.shape
    shift = int(math.log2(width))
    keep = (_iota(shape, 0) >> 3) == (_iota(shape, 1) >> shift)
    return jnp.where(keep, qq, jnp.zeros_like(qq))


SEQ_PER_STEP = 4


def _page_specs(layer, nb, npg, feat, page):
    return [pl.BlockSpec((1, 1, feat, page), lambda b, pt, i=i, j=j: (layer, pt[b * nb + i, j], 0, 0))
            for i in range(nb) for j in range(npg)]


def _per_sequence(body, nb, kinds, npg):
    def kern(pt_ref, *refs):
        n_in = len(kinds)
        pages = refs[n_in:n_in + nb * npg]
        o_ref = refs[n_in + nb * npg]
        scratch = refs[n_in + nb * npg + 1:]
        live = []
        for i in range(nb):
            ins = []
            for kind, r in zip(kinds, refs[:n_in]):
                if kind == "tok":
                    ins.append(r.at[pl.ds(i, 1)])
                elif kind == "win":
                    ins.append(r.at[:, pl.ds(i, 1)])
                else:
                    ins.append(r)
            live.append(body(pt_ref, *ins, *pages[i * npg:(i + 1) * npg], o_ref.at[pl.ds(i, 1)],
                             *[s.at[i] for s in scratch]))
        done = object()
        while live:
            live = [g for g in live if next(g, done) is not done]
    return kern


def _nsa_sample_kernel(pt_ref, q_ref, g_ref, rn_ref, wn_ref, ws_ref, *rest, past, npg, tq, page):
    pages = rest[:npg]
    o_ref = rest[npg]
    s_ref, e_ref = rest[npg + 1], rest[npg + 2]
    del pt_ref
    rows = N_HEADS * QPAD
    nc = past // CMP_BLOCK
    ns = past // SEL_BLOCK + 1
    cpp = page // CMP_BLOCK
    spp = page // SEL_BLOCK

    qp = _pad_rows(q_ref[0], QPAD)
    q4 = jnp.concatenate([qp[:, HEAD_DIM * h:HEAD_DIM * (h + 1)] for h in range(N_HEADS)], axis=0)
    iq = _iota((rows, 1), 0) & (QPAD - 1)
    qpos = past + iq

    tk = _iota((page, nc), 0)
    cb = _iota((page, nc), 1)
    kcvc = jnp.zeros((2 * HEAD_DIM, nc), F32)
    for p in range(npg):
        pm = jnp.where((tk >> 5) + cpp * p == cb, 1.0 / CMP_BLOCK, 0.0).astype(BF16)
        kcvc = kcvc + _dot_split_lhs(pages[p][0, 0, 0:2 * HEAD_DIM, :], pm, 2)
    yield
    kc = kcvc[0:HEAD_DIM].astype(BF16)
    vc = kcvc[HEAD_DIM:2 * HEAD_DIM].astype(BF16)
    cend = (_iota((1, nc), 1) + 1) * CMP_BLOCK - 1
    e, l = _softmax_parts(_dot(q4, kc), cend <= qpos)
    pc = e / l
    oc = _dot_nt(pc.astype(BF16), vc)
    pcs = pc[0:QPAD]
    for h in range(1, N_HEADS):
        pcs = pcs + pc[h * QPAD:(h + 1) * QPAD]
    pcs = jnp.concatenate([pcs] * N_HEADS, axis=0)
    pair = jnp.where((_iota((nc, LANES), 0) >> 1) == _iota((nc, LANES), 1), 1.0, 0.0).astype(BF16)
    imp = _dot_split_lhs(pcs, pair, 3)
    sel = _select_blocks(imp, qpos, ns)

    yield
    lane = _iota((1, page), 1)
    m = jnp.full((rows, 1), NEG_INF, F32)
    for p in range(npg):
        s = _dot(q4, pages[p][0, 0, 2 * HEAD_DIM:3 * HEAD_DIM, :].astype(BF16))
        selk = sel[:, spp * p:spp * p + 1]
        for j in range(1, spp):
            selk = jnp.where(lane >= j * SEL_BLOCK, sel[:, spp * p + j:spp * p + j + 1], selk)
        s = jnp.where(selk > 0.5, s, NEG_INF)
        s_ref[p * rows:(p + 1) * rows, :] = s
        m = jnp.maximum(m, jnp.max(s, axis=-1, keepdims=True))
    yield
    rn = _pad_rows(rn_ref[0], QPAD)
    jn = _iota((1, QPAD), 1)
    nmask = (jn <= iq) & (jn < tq) & (sel[:, ns - 1:ns] > 0.5)
    sn = jnp.where(nmask, _dot_nt(q4, rn[:, 128:192].astype(BF16)), NEG_INF)
    m = jnp.maximum(m, jnp.max(sn, axis=-1, keepdims=True))
    m = jnp.where(m == NEG_INF, 0.0, m)
    en = jnp.exp(sn - m)
    l = jnp.sum(en, axis=-1, keepdims=True)
    acc = _dot(en.astype(BF16), rn[:, 192:256].astype(BF16))
    e = jnp.exp(s_ref[...] - jnp.concatenate([m] * npg, axis=0))
    e_ref[...] = e.astype(BF16)
    lsum = jnp.sum(e, axis=-1, keepdims=True)
    for p in range(npg):
        l = l + lsum[p * rows:(p + 1) * rows]
        acc = acc + _dot_nt(e_ref[p * rows:(p + 1) * rows, :],
                            pages[p][0, 0, 3 * HEAD_DIM:4 * HEAD_DIM, :].astype(BF16))
    o_s = acc / jnp.maximum(l, 1e-30)

    yield
    ws = ws_ref[0, 0]
    wb = ws.shape[1]
    wn = _pad_rows(wn_ref[0], QPAD)
    kwpos = past - wb + _iota((1, wb), 1)
    wmask = (kwpos <= qpos) & (kwpos > qpos - WINDOW) & (kwpos >= 0)
    s1 = jnp.where(wmask, _dot(q4, ws[0:HEAD_DIM, :].astype(BF16)), NEG_INF)
    s2 = jnp.where((jn <= iq) & (jn < tq), _dot_nt(q4, wn[:, 0:HEAD_DIM].astype(BF16)), NEG_INF)
    m = jnp.maximum(jnp.max(s1, axis=-1, keepdims=True), jnp.max(s2, axis=-1, keepdims=True))
    m = jnp.where(m == NEG_INF, 0.0, m)
    e1 = jnp.exp(s1 - m)
    e2 = jnp.exp(s2 - m)
    l = jnp.sum(e1, axis=-1, keepdims=True) + jnp.sum(e2, axis=-1, keepdims=True)
    o_w = (_dot_nt(e1.astype(BF16), ws[HEAD_DIM:2 * HEAD_DIM, :].astype(BF16))
           + _dot(e2.astype(BF16), wn[:, HEAD_DIM:2 * HEAD_DIM].astype(BF16))) / jnp.maximum(l, 1e-30)

    g = _pad_rows(g_ref[0], QPAD)
    outs = []
    for h in range(N_HEADS):
        rs = slice(h * QPAD, (h + 1) * QPAD)
        outs.append(g[:, 3 * h:3 * h + 1] * oc[rs] + g[:, 3 * h + 1:3 * h + 2] * o_s[rs]
                    + g[:, 3 * h + 2:3 * h + 3] * o_w[rs])
    o_ref[0] = jnp.concatenate(outs, axis=1)[0:tq].astype(BF16)


def _nsa_sample(page_table, q, g, rows_new, win_new, win_state, cache, layer):
    db, tq, _ = q.shape
    npg = page_table.shape[1]
    page = cache.shape[3]
    wb = win_state.shape[3]
    rows = N_HEADS * QPAD
    nb = math.gcd(db, SEQ_PER_STEP)
    tok = lambda c: pl.BlockSpec((nb, tq, c), lambda b, pt: (b, 0, 0))
    grid_spec = pltpu.PrefetchScalarGridSpec(
        num_scalar_prefetch=1,
        grid=(db // nb,),
        in_specs=[tok(256), tok(128), tok(256), tok(128),
                  pl.BlockSpec((1, nb, 2 * HEAD_DIM, wb), lambda b, pt: (layer, b, 0, 0))]
        + _page_specs(layer, nb, npg, 4 * HEAD_DIM, page),
        out_specs=tok(256),
        scratch_shapes=[pltpu.VMEM((nb, npg * rows, page), F32), pltpu.VMEM((nb, npg * rows, page), BF16)],
    )
    body = functools.partial(_nsa_sample_kernel, past=npg * page, npg=npg, tq=tq, page=page)
    return pl.pallas_call(
        _per_sequence(body, nb, ("tok", "tok", "tok", "tok", "win"), npg),
        grid_spec=grid_spec,
        out_shape=jax.ShapeDtypeStruct((db, tq, GROUP_W), BF16),
        compiler_params=_params("arbitrary"),
        name="nsa_sample",
    )(page_table, q, g, rows_new, win_new, win_state, *([cache] * (nb * npg)))


def _sb_sample_kernel(pt_ref, q_ref, kvn_ref, *rest, npg, tq, page):
    pages = rest[:npg]
    o_ref = rest[npg]
    z_ref, a_ref = rest[npg + 1], rest[npg + 2]
    del pt_ref
    rows = N_HEADS * QPAD
    qbd = _block_diag_q(q_ref[0], N_HEADS, HEAD_DIM)
    iq = _iota((rows, 1), 0) & (QPAD - 1)

    for p in range(npg):
        z_ref[p * rows:(p + 1) * rows, :] = _dot(qbd, pages[p][0, 0, 0:GROUP_W, :].astype(BF16))
    yield
    z = z_ref[...]
    sp = _softplus(z)
    l1m = -sp
    suf = _dot_split_lhs(l1m, _later_keys(page), 2)
    tot = suf[:, 0:1] + l1m[:, 0:1]

    yield
    kvn = _pad_rows(kvn_ref[0], QPAD)
    jn = _iota((1, QPAD), 1)
    nmask = (jn < iq) & (jn < tq)
    zn = _dot_nt(qbd, kvn[:, 0:GROUP_W].astype(BF16))
    spn = _softplus(zn)
    l1n = jnp.where(nmask, -spn, 0.0)
    sufn = _dot_split_lhs(l1n, _later_keys(QPAD), 2)
    an = jnp.where(nmask, jnp.exp(zn - spn + sufn), 0.0)
    acc = _dot(an.astype(BF16), kvn[:, GROUP_W:2 * GROUP_W].astype(BF16))

    run = jnp.sum(l1n, axis=-1, keepdims=True)
    carries = [None] * npg
    for p in range(npg - 1, -1, -1):
        carries[p] = run
        run = run + tot[p * rows:(p + 1) * rows]
    a_ref[...] = jnp.exp(z - sp + suf + jnp.concatenate(carries, axis=0)).astype(BF16)
    yield
    for p in range(npg):
        acc = acc + _dot_nt(a_ref[p * rows:(p + 1) * rows, :],
                            pages[p][0, 0, GROUP_W:2 * GROUP_W, :].astype(BF16))

    keep = (_iota(acc.shape, 0) >> 3) == (_iota(acc.shape, 1) >> 6)
    acc = jnp.where(keep, acc, 0.0)
    out = acc[0:QPAD]
    for h in range(1, N_HEADS):
        out = out + acc[h * QPAD:(h + 1) * QPAD]
    o_ref[0] = out[0:tq].astype(BF16)


def _sb_sample(page_table, q, kv_new, cache, layer):
    db, tq, _ = q.shape
    npg = page_table.shape[1]
    page = cache.shape[3]
    rows = N_HEADS * QPAD
    nb = math.gcd(db, SEQ_PER_STEP)
    tok = lambda c: pl.BlockSpec((nb, tq, c), lambda b, pt: (b, 0, 0))
    grid_spec = pltpu.PrefetchScalarGridSpec(
        num_scalar_prefetch=1,
        grid=(db // nb,),
        in_specs=[tok(256), tok(512)] + _page_specs(layer, nb, npg, 2 * GROUP_W, page),
        out_specs=tok(256),
        scratch_shapes=[pltpu.VMEM((nb, npg * rows, page), F32), pltpu.VMEM((nb, npg * rows, page), BF16)],
    )
    body = functools.partial(_sb_sample_kernel, npg=npg, tq=tq, page=page)
    return pl.pallas_call(
        _per_sequence(body, nb, ("tok", "tok"), npg),
        grid_spec=grid_spec,
        out_shape=jax.ShapeDtypeStruct((db, tq, GROUP_W), BF16),
        compiler_params=_params("arbitrary"),
        name="sb_sample",
    )(page_table, q, kv_new, *([cache] * (nb * npg)))


def _diff_sample_kernel(pt_ref, q_ref, kvn_ref, lam_ref, gn_ref, *rest, npg, tq, page, lam_init):
    pages = rest[:npg]
    o_ref = rest[npg]
    s_ref, e_ref = rest[npg + 1], rest[npg + 2]
    del pt_ref
    groups = 2 * N_HEADS
    rows = groups * QPAD
    qbd = _block_diag_q(q_ref[0], groups, DIFF_DQ)
    iq = _iota((rows, 1), 0) & (QPAD - 1)

    m = jnp.full((rows, 1), NEG_INF, F32)
    for p in range(npg):
        s = _dot(qbd, pages[p][0, 0, 0:GROUP_W, :].astype(BF16))
        s_ref[p * rows:(p + 1) * rows, :] = s
        m = jnp.maximum(m, jnp.max(s, axis=-1, keepdims=True))
    yield
    kvn = _pad_rows(kvn_ref[0], QPAD)
    jn = _iota((1, QPAD), 1)
    nmask = (jn <= iq) & (jn < tq)
    sn = jnp.where(nmask, _dot_nt(qbd, kvn[:, 0:GROUP_W].astype(BF16)), NEG_INF)
    m = jnp.maximum(m, jnp.max(sn, axis=-1, keepdims=True))
    en = jnp.exp(sn - m)
    l = jnp.sum(en, axis=-1, keepdims=True)
    acc = _dot(en.astype(BF16), kvn[:, GROUP_W:2 * GROUP_W].astype(BF16))
    e = jnp.exp(s_ref[...] - jnp.concatenate([m] * npg, axis=0))
    e_ref[...] = e.astype(BF16)
    lsum = jnp.sum(e, axis=-1, keepdims=True)
    for p in range(npg):
        l = l + lsum[p * rows:(p + 1) * rows]
        acc = acc + _dot_nt(e_ref[p * rows:(p + 1) * rows, :],
                            pages[p][0, 0, GROUP_W:2 * GROUP_W, :].astype(BF16))
    o = acc / jnp.maximum(l, 1e-30)

    yield
    lam = _diff_lambda(lam_ref, lam_init)
    gn = gn_ref[...]
    outs = []
    for h in range(N_HEADS):
        cs = slice(HEAD_DIM * h, HEAD_DIM * (h + 1))
        o1 = o[(2 * h) * QPAD:(2 * h + 1) * QPAD, cs]
        o2 = o[(2 * h + 1) * QPAD:(2 * h + 2) * QPAD, cs]
        outs.append(_diff_finish(o1, o2, lam, gn, lam_init))
    o_ref[0] = jnp.concatenate(outs, axis=1)[0:tq].astype(BF16)


def _diff_sample(page_table, q, kv_new, lam, gn, cache, layer, lam_init):
    db, tq, _ = q.shape
    npg = page_table.shape[1]
    page = cache.shape[3]
    rows = 2 * N_HEADS * QPAD
    nb = math.gcd(db, SEQ_PER_STEP)
    tok = lambda c: pl.BlockSpec((nb, tq, c), lambda b, pt: (b, 0, 0))
    grid_spec = pltpu.PrefetchScalarGridSpec(
        num_scalar_prefetch=1,
        grid=(db // nb,),
        in_specs=[tok(256), tok(512),
                  pl.BlockSpec((4, DIFF_DQ), lambda b, pt: (0, 0)),
                  pl.BlockSpec((1, HEAD_DIM), lambda b, pt: (0, 0))]
        + _page_specs(layer, nb, npg, 2 * GROUP_W, page),
        out_specs=tok(256),
        scratch_shapes=[pltpu.VMEM((nb, npg * rows, page), F32), pltpu.VMEM((nb, npg * rows, page), BF16)],
    )
    body = functools.partial(_diff_sample_kernel, npg=npg, tq=tq, page=page, lam_init=lam_init)
    return pl.pallas_call(
        _per_sequence(body, nb, ("tok", "tok", "const", "const"), npg),
        grid_spec=grid_spec,
        out_shape=jax.ShapeDtypeStruct((db, tq, GROUP_W), BF16),
        compiler_params=_params("arbitrary"),
        name="diff_sample",
    )(page_table, q, kv_new, lam, gn, *([cache] * (nb * npg)))


def _mix_residual(x_ref, oa_ref, ob_ref, oc_ref, od_ref, wo_ref):
    mixed = jnp.concatenate([oa_ref[...], ob_ref[...], oc_ref[...], od_ref[...]], axis=1)
    return x_ref[...] + _dot(mixed, wo_ref[...])


def _swiglu_act(h, wg_ref, wu_ref, act_ref, width, chunk, lead=()):
    for c0 in range(0, width, chunk):
        c1 = min(c0 + chunk, width)
        idx = lead + (slice(None), slice(c0, c1))
        gate = _dot(h, wg_ref[idx])
        up = _dot(h, wu_ref[idx])
        act_ref[:, c0:c1] = (_silu(gate) * up).astype(BF16)


def _ffn_kernel(x_ref, oa_ref, ob_ref, oc_ref, od_ref, wo_ref, g_ref, wg_ref, wu_ref, wd_ref,
                o_ref, act_ref):
    x = _mix_residual(x_ref, oa_ref, ob_ref, oc_ref, od_ref, wo_ref)
    h = _rmsnorm(x, g_ref[...]).astype(BF16)
    _swiglu_act(h, wg_ref, wu_ref, act_ref, wg_ref.shape[1], 512)
    o_ref[...] = x + _dot(act_ref[...], wd_ref[...])


def _ffn(x, o4, wo, g, wg, wu, wd, tm):
    n = x.shape[0]
    f = wg.shape[1]
    row = lambda c: pl.BlockSpec((tm, c), lambda i: (i, 0))
    return pl.pallas_call(
        _ffn_kernel,
        grid=(n // tm,),
        in_specs=[row(D_MODEL)] + [row(GROUP_W)] * 4
        + [_const_spec((D_MODEL, D_MODEL)), _const_spec((1, D_MODEL)),
           _const_spec((D_MODEL, f)), _const_spec((D_MODEL, f)), _const_spec((f, D_MODEL))],
        out_specs=row(D_MODEL),
        out_shape=jax.ShapeDtypeStruct((n, D_MODEL), F32),
        scratch_shapes=[pltpu.VMEM((tm, f), BF16)],
        compiler_params=_params("parallel"),
        name="ffn",
    )(x, *o4, wo, g, wg, wu, wd)


def _moe_kernel(x_ref, oa_ref, ob_ref, oc_ref, od_ref, wo_ref, g_ref, wr_ref, wg_ref, wu_ref, wd_ref,
                o_ref, h_ref, gate_ref, acc_ref, act_ref):
    e = pl.program_id(1)
    lane = _iota((1, LANES), 1)

    @pl.when(e == 0)
    def _():
        x = _mix_residual(x_ref, oa_ref, ob_ref, oc_ref, od_ref, wo_ref)
        o_ref[...] = x
        h = _rmsnorm(x, g_ref[...]).astype(BF16)
        h_ref[...] = h
        lanef = lane.astype(F32)
        logits = jnp.where(lane < N_EXPERTS, _dot(h, wr_ref[...]), NEG_INF)
        v1 = jnp.max(logits, axis=-1, keepdims=True)
        i1 = jnp.min(jnp.where(logits == v1, lanef, float(LANES)), axis=-1, keepdims=True)
        rest = jnp.where(lanef == i1, NEG_INF, logits)
        v2 = jnp.max(rest, axis=-1, keepdims=True)
        i2 = jnp.min(jnp.where(rest == v2, lanef, float(LANES)), axis=-1, keepdims=True)
        t = jnp.exp(v2 - v1)
        gate_ref[...] = (jnp.where(lanef == i1, 1.0 / (1.0 + t), 0.0)
                         + jnp.where(lanef == i2, t / (1.0 + t), 0.0))
        acc_ref[...] = jnp.zeros(acc_ref.shape, F32)

    gcol = jnp.sum(jnp.where(lane == e, gate_ref[...], 0.0), axis=-1, keepdims=True)
    _swiglu_act(h_ref[...], wg_ref, wu_ref, act_ref, wg_ref.shape[2], 512, lead=(0,))
    acc_ref[...] += gcol * _dot(act_ref[...], wd_ref[0])

    @pl.when(e == pl.num_programs(1) - 1)
    def _():
        o_ref[...] += acc_ref[...]


def _moe(x, o4, wo, g, wr, wg, wu, wd, tm):
    n = x.shape[0]
    ne, _, f = wg.shape
    row = lambda c: pl.BlockSpec((tm, c), lambda i, e: (i, 0))
    const = lambda shape: pl.BlockSpec(shape, lambda i, e: (0,) * len(shape))
    return pl.pallas_call(
        _moe_kernel,
        grid=(n // tm, ne),
        in_specs=[pl.BlockSpec((tm, D_MODEL), lambda i, e: (i, 0), pipeline_mode=pl.Buffered(1))]
        + [row(GROUP_W)] * 4
        + [const((D_MODEL, D_MODEL)), const((1, D_MODEL)), const((D_MODEL, LANES)),
           pl.BlockSpec((1, D_MODEL, f), lambda i, e: (e, 0, 0)),
           pl.BlockSpec((1, D_MODEL, f), lambda i, e: (e, 0, 0)),
           pl.BlockSpec((1, f, D_MODEL), lambda i, e: (e, 0, 0))],
        out_specs=row(D_MODEL),
        out_shape=jax.ShapeDtypeStruct((n, D_MODEL), F32),
        scratch_shapes=[pltpu.VMEM((tm, D_MODEL), BF16),
                        pltpu.VMEM((tm, LANES), F32), pltpu.VMEM((tm, D_MODEL), F32),
                        pltpu.VMEM((tm, f), BF16)],
        compiler_params=_params("parallel", "arbitrary"),
        name="moe",
    )(x, *o4, wo, g, wr, wg, wu, wd)


def _final_norm_kernel(x_ref, g_ref, o_ref):
    o_ref[...] = _rmsnorm(x_ref[...], g_ref[...])


def _final_norm(x, g, tm):
    n = x.shape[0]
    row = pl.BlockSpec((tm, D_MODEL), lambda i: (i, 0))
    return pl.pallas_call(
        _final_norm_kernel,
        grid=(n // tm,),
        in_specs=[row, _const_spec((1, D_MODEL))],
        out_specs=row,
        out_shape=jax.ShapeDtypeStruct((n, D_MODEL), F32),
        compiler_params=_params("parallel"),
        name="final_norm",
    )(x, g)


def _row_tile(n, cap):
    t = min(n, cap)
    while n % t:
        t //= 2
    return t


def _tokens_minor(a, lead):
    nd = a.ndim
    perm = tuple(range(lead)) + tuple(range(lead + 1, nd)) + (lead,)
    t = jnp.transpose(a, perm)
    return t.reshape(t.shape[:lead] + (-1, t.shape[-1]))


@jax.jit
def kernel(x_prompt, x_sample, cache_nsa, cache_sb, cache_diff, state_nsa_win, state_pool, page_table,
           norm_mix, w_in, w_out, diff_lambda, diff_norm, pool_w, pool_scale, norm_ffn,
           ffn_gate, ffn_up, ffn_down, router_w, moe_gate, moe_up, moe_down, norm_final):
    b, s, d = x_prompt.shape
    db, t, _ = x_sample.shape
    depth, n_pool, page = cache_nsa.shape[:3]
    past = page_table.shape[1] * page
    wb = state_nsa_win.shape[2]
    wkeep = min(WINDOW, s)
    np_, ns_ = b * s, db * t
    tm_p = _row_tile(s, 512)
    tm_s = _row_tile(ns_, 512)

    tabs_p = _rope_tables(jnp.arange(s))
    tabs_s = _rope_tables(past + (jnp.arange(tm_s) % t))
    cache_nsa_t = _tokens_minor(cache_nsa, 2)
    cache_sb_t = _tokens_minor(cache_sb, 2)
    cache_diff_t = _tokens_minor(cache_diff, 2)
    state_win_t = _tokens_minor(state_nsa_win, 2)

    xp = x_prompt.reshape(np_, d)
    xs = x_sample.reshape(ns_, d)
    outs = [[] for _ in range(10)]
    for l in range(depth):
        lam_init = 0.8 - 0.6 * math.exp(-0.3 * l)
        g_mix = norm_mix[l].reshape(1, d)
        w_l = _prep_w_in(w_in[l])
        lam_l = diff_lambda[l]
        gn_l = diff_norm[l].reshape(1, HEAD_DIM)
        wp_l = _pool_block_diag(pool_w[l])
        ps_l = pool_scale[l].reshape(1, GROUP_W)

        qa, rows, win, gate, qb, kvb, qc, kvc, du = _project(xp, g_mix, w_l, tabs_p, tm_p)
        qa_s, rows_s, win_s, gate_s, qb_s, kvb_s, qc_s, kvc_s, du_s = _project(xs, g_mix, w_l, tabs_s, tm_s)

        p3 = lambda a: a.reshape(b, s, a.shape[-1])
        s3 = lambda a: a.reshape(db, t, a.shape[-1])
        oa = _nsa_prompt_t(p3(qa), p3(gate), p3(rows), p3(win))
        ob = _sb_prompt(p3(qb), p3(kvb))
        oc = _diff_prompt(p3(qc), p3(kvc), lam_l, gn_l, lam_init)
        od = _pool(p3(du), wp_l, ps_l, 0, 0)

        oa_s = _nsa_sample(page_table, s3(qa_s), s3(gate_s), s3(rows_s), s3(win_s), state_win_t,
                           cache_nsa_t, l)
        ob_s = _sb_sample(page_table, s3(qb_s), s3(kvb_s), cache_sb_t, l)
        oc_s = _diff_sample(page_table, s3(qc_s), s3(kvc_s), lam_l, gn_l, cache_diff_t, l, lam_init)
        u_ext = jnp.concatenate([state_pool[l], s3(du_s)], axis=1)
        od_s = _pool(u_ext, wp_l, ps_l, POOL_HIST, past)

        outs[0].append(rows.reshape(b, s, 4, HEAD_DIM))
        outs[1].append(rows_s.reshape(db, t, 4, HEAD_DIM))
        outs[2].append(kvb.reshape(b, s, 2, N_HEADS, HEAD_DIM))
        outs[3].append(kvb_s.reshape(db, t, 2, N_HEADS, HEAD_DIM))
        outs[4].append(kvc.reshape(b, s, 2, N_HEADS, HEAD_DIM))
        outs[5].append(kvc_s.reshape(db, t, 2, N_HEADS, HEAD_DIM))
        outs[6].append(win.reshape(b, s, 2, HEAD_DIM)[:, s - wkeep:])
        outs[7].append(jnp.concatenate([state_nsa_win[l], win_s.reshape(db, t, 2, HEAD_DIM)], axis=1)[:, -wb:])
        outs[8].append(p3(du)[:, s - POOL_HIST:])
        outs[9].append(u_ext[:, -POOL_HIST:])

        o4_p = [a.reshape(np_, GROUP_W) for a in (oa, ob, oc, od)]
        o4_s = [a.reshape(ns_, GROUP_W) for a in (oa_s, ob_s, oc_s, od_s)]
        wo = w_out[l].astype(BF16)
        g_ffn = norm_ffn[l].reshape(1, d)
        i = l // 2
        if l % 2 == 0:
            wg, wu, wd = ffn_gate[i].astype(BF16), ffn_up[i].astype(BF16), ffn_down[i].astype(BF16)
            xp = _ffn(xp, o4_p, wo, g_ffn, wg, wu, wd, tm_p)
            xs = _ffn(xs, o4_s, wo, g_ffn, wg, wu, wd, tm_s)
        else:
            wr = jnp.pad(router_w[i], ((0, 0), (0, LANES - N_EXPERTS))).astype(BF16)
            wg, wu, wd = moe_gate[i].astype(BF16), moe_up[i].astype(BF16), moe_down[i].astype(BF16)
            xp = _moe(xp, o4_p, wo, g_ffn, wr, wg, wu, wd, _row_tile(s, 1024))
            xs = _moe(xs, o4_s, wo, g_ffn, wr, wg, wu, wd, tm_s)

    g_fin = norm_final.reshape(1, d)
    y_p = _final_norm(xp, g_fin, tm_p).reshape(b, s, d)
    y_s = _final_norm(xs, g_fin, tm_s).reshape(db, t, d)
    return (y_p, y_s) + tuple(jnp.stack(o) for o in outs)
```
